```python
import math
import jax, jax.numpy as jnp
from jax import lax
import numpy as np

D_MODEL = 2048
BATCH = 4
SEQ = 2048
DEPTH = 1
DEC_BATCH = 128
DEC_SEQ = 8
PAST_LEN = 16384
PAGE_SIZE = 128

MIX_WIDTH = D_MODEL
H_A = 8
DK_A = MIX_WIDTH // 2 // H_A
DV_A = MIX_WIDTH // 2 // H_A
H_B = 8
DK_B = MIX_WIDTH // 2 // H_B
DV_B = MIX_WIDTH // 2 // H_B
CONV_W = 4
C_CONV = 2 * H_B * DK_B + H_B * DV_B
D_FF = 4 * D_MODEL
CHUNK = 64
EPS = 1e-6
COL_SIZES = (H_A * DK_A, H_A * DK_A, H_A * DV_A, H_A, H_A, H_A * DV_A,
             H_B * DK_B, H_B * DK_B, H_B * DV_B, H_B, H_B, H_B * DV_B)
P_IN = sum(COL_SIZES)

kernel_name = "hymba_mlstm_gdn_adaln_decoder_step"


def _rmsnorm(x, g):
    x32 = x.astype(jnp.float32)
    y = x32 * lax.rsqrt(jnp.mean(x32 * x32, axis=-1, keepdims=True) + EPS)
    return y.astype(x.dtype) * g


def _head_rmsnorm(h, g):
    B, T, H, E = h.shape
    y = h * lax.rsqrt(jnp.mean(h * h, axis=-1, keepdims=True) + EPS)
    return (y * g.astype(jnp.float32).reshape(H, E)).reshape(B, T, H * E)


def _l2norm(x):
    return x * lax.rsqrt(jnp.sum(x * x, axis=-1, keepdims=True) + EPS)


def _modulate(h, shift, scale):
    return h * (1.0 + scale[:, None, :]) + shift[:, None, :]


def _split_cols(p):
    idx = []
    acc = 0
    for s in COL_SIZES[:-1]:
        acc += s
        idx.append(acc)
    return jnp.split(p, idx, axis=-1)


def _to_chunks(a, L):
    B, T, H = a.shape[:3]
    rest = a.shape[3:]
    a = a.reshape((B, T // L, L, H) + rest)
    return a.transpose((1, 0, 3, 2) + tuple(range(4, a.ndim)))


def _from_chunks(a):
    N, B, H, L, E = a.shape
    return a.transpose(1, 0, 3, 2, 4).reshape(B, N * L, H, E)


def _causal_conv(u, buf, w):
    T = u.shape[1]
    full = jnp.concatenate([buf.astype(u.dtype), u], axis=1)
    out = full[:, 0:T] * w[0]
    for j in range(1, CONV_W):
        out = out + full[:, j:j + T] * w[j]
    return out, full[:, -(CONV_W - 1):]


def _mlstm_scan(q, k, v, li, lf, C0, n0, m0):
    T = q.shape[1]
    L = math.gcd(T, CHUNK)
    xs = (_to_chunks(q, L), _to_chunks(k, L), _to_chunks(v, L),
          _to_chunks(li, L), _to_chunks(lf, L))
    causal = jnp.tril(jnp.ones((L, L), dtype=bool))

    def step(carry, xc):
        C, n, m = carry
        qc, kc, vc, lic, lfc = xc
        b = jnp.cumsum(lfc, axis=-1)
        d = jnp.where(causal, b[..., :, None] - b[..., None, :] + lic[..., None, :], -jnp.inf)
        inter = b + m[..., None]
        m_t = jnp.maximum(inter, jnp.max(d, axis=-1))
        w = jnp.exp(d - m_t[..., None])
        dec = jnp.exp(inter - m_t)
        s = jnp.einsum('bhtd,bhsd->bhts', qc, kc) * w
        numer = dec[..., None] * jnp.einsum('bhtd,bhde->bhte', qc, C) + jnp.einsum('bhts,bhse->bhte', s, vc)
        dd = dec * jnp.einsum('bhtd,bhd->bht', qc, n) + jnp.sum(s, axis=-1)
        h = numer / jnp.maximum(jnp.abs(dd), jnp.exp(-m_t))[..., None]
        m_new = m_t[..., -1]
        wl = jnp.exp(b[..., -1:] - b + lic - m_new[..., None])
        dC = jnp.exp(b[..., -1] + m - m_new)
        C_new = dC[..., None, None] * C + jnp.einsum('bhs,bhsd,bhse->bhde', wl, kc, vc)
        n_new = dC[..., None] * n + jnp.einsum('bhs,bhsd->bhd', wl, kc)
        return (C_new, n_new, m_new), h

    (C, n, m), hs = lax.scan(step, (C0, n0, m0), xs)
    return _from_chunks(hs), C, n, m


def _gdn_scan(q, k, v, logg, beta, S0):
    T = q.shape[1]
    L = math.gcd(T, CHUNK)
    xs = (_to_chunks(q, L), _to_chunks(k, L), _to_chunks(v, L),
          _to_chunks(logg, L), _to_chunks(beta, L))
    incl = jnp.tril(jnp.ones((L, L), dtype=bool))
    strict = jnp.tril(jnp.ones((L, L), dtype=bool), -1)
    eye = jnp.eye(L, dtype=jnp.float32)

    def step(S, xc):
        qc, kc, vc, gc, bc = xc
        G = jnp.cumsum(gc, axis=-1)
        decay = jnp.exp(jnp.where(incl, G[..., :, None] - G[..., None, :], -jnp.inf))
        kk = jnp.einsum('bhtd,bhsd->bhts', kc, kc)
        a = jnp.where(strict, bc[..., None] * decay * kk, 0.0)
        gam = jnp.exp(G)
        rhs = bc[..., None] * (vc - gam[..., None] * jnp.einsum('bhtd,bhde->bhte', kc, S))
        u = lax.linalg.triangular_solve(a + eye, rhs, left_side=True, lower=True, unit_diagonal=True)
        qk = jnp.einsum('bhtd,bhsd->bhts', qc, kc) * decay
        o = gam[..., None] * jnp.einsum('bhtd,bhde->bhte', qc, S) + jnp.einsum('bhts,bhse->bhte', qk, u)
        S_new = jnp.exp(G[..., -1])[..., None, None] * S + jnp.einsum(
            'bhs,bhsd,bhse->bhde', jnp.exp(G[..., -1:] - G), kc, u)
        return S_new, o

    S, os_ = lax.scan(step, S0, xs)
    return _from_chunks(os_), S


def _layer(x, c, C0, n0, m0, S0, conv0, ada_w, ada_b, norm1, w_in, gate_b, mlstm_g,
           conv_w, A_log, dt_bias, gdn_g, w_out, norm2, w_up, w_down):
    f32 = jnp.float32
    B, T, _ = x.shape
    mod = jax.nn.silu(c) @ ada_w + ada_b
    sh1, sc1, g1, sh2, sc2, g2 = jnp.split(mod, 6, axis=-1)
    h = _modulate(_rmsnorm(x, norm1), sh1, sc1)
    proj = h @ w_in
    qa, ka, va, ia, fa, oa, qb, kb, vb, ab, bb, zb = _split_cols(proj)

    qa = qa.astype(f32).reshape(B, T, H_A, DK_A)
    ka = ka.astype(f32).reshape(B, T, H_A, DK_A) * (DK_A ** -0.5)
    va = va.astype(f32).reshape(B, T, H_A, DV_A)
    gb = gate_b.astype(f32)
    li = ia.astype(f32) + gb[:H_A]
    lf = jax.nn.log_sigmoid(fa.astype(f32) + gb[H_A:])
    ha, C1, n1, m1 = _mlstm_scan(qa, ka, va, li, lf, C0.astype(f32), n0.astype(f32), m0.astype(f32))
    ha = _head_rmsnorm(ha, mlstm_g) * jax.nn.sigmoid(oa.astype(f32))

    qkv = jnp.concatenate([qb, kb, vb], axis=-1)
    qkv, conv1 = _causal_conv(qkv, conv0, conv_w)
    qkv = jax.nn.silu(qkv.astype(f32))
    qb, kb, vb = jnp.split(qkv, [H_B * DK_B, 2 * H_B * DK_B], axis=-1)
    qb = _l2norm(qb.reshape(B, T, H_B, DK_B)) * (DK_B ** -0.5)
    kb = _l2norm(kb.reshape(B, T, H_B, DK_B))
    vb = vb.reshape(B, T, H_B, DV_B)
    logg = -jnp.exp(A_log.astype(f32)) * jax.nn.softplus(ab.astype(f32) + dt_bias.astype(f32))
    beta = jax.nn.sigmoid(bb.astype(f32))
    hb, S1 = _gdn_scan(qb, kb, vb, logg, beta, S0.astype(f32))
    hb = _head_rmsnorm(hb, gdn_g) * jax.nn.silu(zb.astype(f32))

    mix = jnp.concatenate([ha, hb], axis=-1).astype(x.dtype) @ w_out
    x = x + g1[:, None, :] * mix
    h2 = _modulate(_rmsnorm(x, norm2), sh2, sc2)
    x = x + g2[:, None, :] * (jnp.square(jax.nn.relu(h2 @ w_up)) @ w_down)
    return x, C1, n1, m1, S1, conv1


def _final(x, c, w, b, g):
    sh, sc = jnp.split(jax.nn.silu(c) @ w + b, 2, axis=-1)
    return _modulate(_rmsnorm(x, g), sh, sc)


def setup_inputs(seed: int = 0) -> dict:
    key = jax.random.key(seed)
    ks = jax.random.split(key, 32)
    nrm = jax.random.normal
    D = D_MODEL
    f_bias = jnp.linspace(3.0, 6.0, H_A)[None, :] + 0.1 * nrm(ks[14], (DEPTH, H_A))
    i_bias = 0.1 * nrm(ks[15], (DEPTH, H_A))
    dt = jnp.exp(jax.random.uniform(ks[18], (DEPTH, H_B), minval=math.log(1e-3), maxval=math.log(1e-1)))
    return {
        "x_prompt": nrm(ks[0], (BATCH, SEQ, D)),
        "x_sample": nrm(ks[1], (DEC_BATCH, DEC_SEQ, D)),
        "state_mlstm_C": nrm(ks[2], (DEPTH, DEC_BATCH, H_A, DK_A, DV_A)) * DK_A ** -0.5,
        "state_mlstm_n": nrm(ks[3], (DEPTH, DEC_BATCH, H_A, DK_A)) * 0.5,
        "state_mlstm_m": nrm(ks[4], (DEPTH, DEC_BATCH, H_A)),
        "state_gdn_S": nrm(ks[5], (DEPTH, DEC_BATCH, H_B, DK_B, DV_B)) * DK_B ** -0.5,
        "state_gdn_conv": nrm(ks[6], (DEPTH, DEC_BATCH, CONV_W - 1, C_CONV)),
        "c_prompt": nrm(ks[7], (BATCH, D)),
        "c_sample": nrm(ks[8], (DEC_BATCH, D)),
        "ada_w": nrm(ks[9], (DEPTH, D, 6 * D)) * (0.5 * D ** -0.5),
        "ada_b": nrm(ks[10], (DEPTH, 6 * D)) * 0.02,
        "norm1": 1.0 + 0.01 * nrm(ks[11], (DEPTH, D)),
        "w_in": nrm(ks[12], (DEPTH, D, P_IN)) * D ** -0.5,
        "mlstm_gate_bias": jnp.concatenate([i_bias, f_bias], axis=-1),
        "mlstm_norm": 1.0 + 0.01 * nrm(ks[13], (DEPTH, H_A * DV_A)),
        "gdn_conv_w": nrm(ks[16], (DEPTH, CONV_W, C_CONV)) * CONV_W ** -0.5,
        "gdn_A_log": jnp.log(jax.random.uniform(ks[17], (DEPTH, H_B), minval=1.0, maxval=16.0)),
        "gdn_dt_bias": dt + jnp.log(-jnp.expm1(-dt)),
        "gdn_norm": 1.0 + 0.01 * nrm(ks[19], (DEPTH, H_B * DV_B)),
        "w_out": nrm(ks[20], (DEPTH, MIX_WIDTH, D)) * MIX_WIDTH ** -0.5,
        "norm2": 1.0 + 0.01 * nrm(ks[21], (DEPTH, D)),
        "w_up": nrm(ks[22], (DEPTH, D, D_FF)) * D ** -0.5,
        "w_down": nrm(ks[23], (DEPTH, D_FF, D)) * D_FF ** -0.5,
        "ada_final_w": nrm(ks[24], (D, 2 * D)) * (0.5 * D ** -0.5),
        "ada_final_b": nrm(ks[25], (2 * D,)) * 0.02,
        "norm_final": 1.0 + 0.01 * nrm(ks[26], (D,)),
    }


def reference(x_prompt, x_sample, state_mlstm_C, state_mlstm_n, state_mlstm_m, state_gdn_S,
              state_gdn_conv, c_prompt, c_sample, ada_w, ada_b, norm1, w_in, mlstm_gate_bias,
              mlstm_norm, gdn_conv_w, gdn_A_log, gdn_dt_bias, gdn_norm, w_out, norm2, w_up,
              w_down, ada_final_w, ada_final_b, norm_final):
    f32 = jnp.float32
    Bp = x_prompt.shape[0]
    xp, xs = x_prompt, x_sample
    pC, pn, pm, pS, pconv = [], [], [], [], []
    sC, sn, sm, sS, sconv = [], [], [], [], []
    for l in range(DEPTH):
        w_l = (ada_w[l], ada_b[l], norm1[l], w_in[l], mlstm_gate_bias[l], mlstm_norm[l],
               gdn_conv_w[l], gdn_A_log[l], gdn_dt_bias[l], gdn_norm[l], w_out[l], norm2[l],
               w_up[l], w_down[l])
        xp, C1, n1, m1, S1, cv1 = _layer(
            xp, c_prompt,
            jnp.zeros((Bp, H_A, DK_A, DV_A), f32), jnp.zeros((Bp, H_A, DK_A), f32),
            jnp.zeros((Bp, H_A), f32), jnp.zeros((Bp, H_B, DK_B, DV_B), f32),
            jnp.zeros((Bp, CONV_W - 1, C_CONV), xp.dtype), *w_l)
        pC.append(C1); pn.append(n1); pm.append(m1); pS.append(S1); pconv.append(cv1)
        xs, C2, n2, m2, S2, cv2 = _layer(
            xs, c_sample, state_mlstm_C[l], state_mlstm_n[l], state_mlstm_m[l],
            state_gdn_S[l], state_gdn_conv[l], *w_l)
        sC.append(C2); sn.append(n2); sm.append(m2); sS.append(S2); sconv.append(cv2)
    y_prompt = _final(xp, c_prompt, ada_final_w, ada_final_b, norm_final)
    y_sample = _final(xs, c_sample, ada_final_w, ada_final_b, norm_final)
    return (y_prompt, y_sample,
            jnp.stack(pC), jnp.stack(pn), jnp.stack(pm), jnp.stack(pS), jnp.stack(pconv),
            jnp.stack(sC), jnp.stack(sn), jnp.stack(sm), jnp.stack(sS), jnp.stack(sconv))
```

```python
import functools
import math

import jax
import jax.numpy as jnp
from jax import lax
from jax.experimental import pallas as pl
from jax.experimental.pallas import tpu as pltpu

F32 = jnp.float32
BF16 = jnp.bfloat16

D_MODEL = 2048
N_HEADS = 8
D_HEAD = 128
HALF = N_HEADS * D_HEAD
CONV_W = 4
C_CONV = 3 * HALF
D_FF = 4 * D_MODEL
CHUNK = 64
EPS = 1e-6
N_GATE = 2 * N_HEADS

SUBLANES = 8
LANES = 128
VMEM_LIMIT = 56 * 1024 * 1024

NT = (((1,), (1,)), ((), ()))
TN = (((0,), (0,)), ((), ()))


def _params(sem):
    return pltpu.CompilerParams(dimension_semantics=sem, vmem_limit_bytes=VMEM_LIMIT)


def _silu(x):
    return x * jax.nn.sigmoid(x)


def _softplus(x):
    return jnp.maximum(x, 0.0) + jnp.log1p(jnp.exp(-jnp.abs(x)))


def _log_sigmoid(x):
    return jnp.minimum(x, 0.0) - jnp.log1p(jnp.exp(-jnp.abs(x)))


def _rms(x, g):
    return x * lax.rsqrt(jnp.mean(x * x, axis=-1, keepdims=True) + EPS) * g


def _bdot(a, b):
    return jnp.dot(a.astype(BF16), b.astype(BF16), preferred_element_type=F32)


def _bdot_g(a, b, dims):
    return lax.dot_general(a.astype(BF16), b.astype(BF16), dims, preferred_element_type=F32)


def _fdot(a, b):
    return jnp.dot(a, b, preferred_element_type=F32, precision=lax.Precision.HIGHEST)


def _rows(ref, r0, rows, bb):
    if bb == 1:
        return ref[0, pl.ds(r0, rows), :]
    nb = rows // SUBLANES
    b0 = pl.multiple_of(r0 // SUBLANES, nb)
    return ref[pl.ds(b0, nb), :, :].reshape(rows, ref.shape[-1])


def _mod_rows(ref, r0, rows, bb):
    if bb == 1:
        return ref[0]
    nb = rows // SUBLANES
    b0 = pl.multiple_of(r0 // SUBLANES, nb)
    v = ref[pl.ds(b0, nb), :, :]
    return jnp.broadcast_to(v, (nb, SUBLANES, v.shape[-1])).reshape(rows, v.shape[-1])


def _store_rows(ref, r0, rows, bb, val):
    if bb == 1:
        ref[0, pl.ds(r0, rows), :] = val.astype(ref.dtype)
    else:
        nb = rows // SUBLANES
        b0 = pl.multiple_of(r0 // SUBLANES, nb)
        ref[pl.ds(b0, nb), :, :] = val.reshape(nb, SUBLANES, val.shape[-1]).astype(ref.dtype)


def _tile_maps(T, tt):
    tpb = T // tt
    x_map = lambda m, *_: (m // tpb, m % tpb, 0)
    mod_map = lambda m, *_: (m // tpb, 0, 0)
    return x_map, mod_map


ROW_CHUNK = 256


def _ada_kernel(c_ref, w_ref, b_ref, o_ref):
    s = _silu(c_ref[...])
    o_ref[...] = _bdot(s, w_ref[...]) + b_ref[...]


def _ada(c, w, b, tn=1024):
    M, K = c.shape
    N = w.shape[1]
    return pl.pallas_call(
        _ada_kernel,
        grid=(N // tn,),
        in_specs=[pl.BlockSpec((M, K), lambda n: (0, 0)),
                  pl.BlockSpec((K, tn), lambda n: (0, n)),
                  pl.BlockSpec((1, tn), lambda n: (0, n))],
        out_specs=pl.BlockSpec((M, tn), lambda n: (0, n)),
        out_shape=jax.ShapeDtypeStruct((M, N), F32),
        compiler_params=_params(("arbitrary",)),
        name="ada",
    )(c, w, b.reshape(1, N))


def _inproj_kernel(x_ref, sh_ref, sc_ref, g_ref, w_ref, wg_ref, o_ref, og_ref, h_scr, *, bb, tm):
    @pl.when(pl.program_id(1) == 0)
    def _():
        def body(i, carry):
            r0 = pl.multiple_of(i * ROW_CHUNK, ROW_CHUNK)
            x = _rows(x_ref, r0, ROW_CHUNK, bb)
            h = _rms(x, g_ref[...]) * (1.0 + _mod_rows(sc_ref, r0, ROW_CHUNK, bb)) \
                + _mod_rows(sh_ref, r0, ROW_CHUNK, bb)
            h_scr[pl.ds(r0, ROW_CHUNK), :] = h.astype(BF16)
            return carry
        lax.fori_loop(0, tm // ROW_CHUNK, body, 0)
        og_ref[...] = jnp.dot(h_scr[...], wg_ref[...], preferred_element_type=F32)

    o_ref[...] = jnp.dot(h_scr[...], w_ref[...], preferred_element_type=F32).astype(o_ref.dtype)


def _inproj(x, sh, sc, g, w, wg, bb, tt, out_dtype, tn=512):
    B, T, D = x.shape
    tm = bb * tt
    n_m = (B * T) // tm
    N = w.shape[1]
    x_map, mod_map = _tile_maps(T, tt)
    return pl.pallas_call(
        functools.partial(_inproj_kernel, bb=bb, tm=tm),
        grid=(n_m, N // tn),
        in_specs=[pl.BlockSpec((bb, tt, D), x_map),
                  pl.BlockSpec((bb, 1, D), mod_map),
                  pl.BlockSpec((bb, 1, D), mod_map),
                  pl.BlockSpec((1, D), lambda m, n: (0, 0)),
                  pl.BlockSpec((D, tn), lambda m, n: (0, n)),
                  pl.BlockSpec((D, LANES), lambda m, n: (0, 0))],
        out_specs=[pl.BlockSpec((tm, tn), lambda m, n: (m, n)),
                   pl.BlockSpec((tm, LANES), lambda m, n: (m, 0))],
        out_shape=[jax.ShapeDtypeStruct((B * T, N), out_dtype),
                   jax.ShapeDtypeStruct((B * T, LANES), F32)],
        scratch_shapes=[pltpu.VMEM((tm, D), BF16)],
        compiler_params=_params(("arbitrary", "arbitrary")),
        name="inproj",
    )(x, sh, sc, g, w, wg)


def _masks(L):
    row = lax.broadcasted_iota(jnp.int32, (L, L), 0)
    col = lax.broadcasted_iota(jnp.int32, (L, L), 1)
    return col <= row, col < row


def _sub_row0(j, L):
    return j * L if isinstance(j, int) else pl.multiple_of(j * L, L)


def _head_norm_gate(h, g, gate):
    return h * lax.rsqrt(jnp.mean(h * h, axis=-1, keepdims=True) + EPS) * g * gate


def _mlstm_head(q, k, v, li_r, lf_r, li_c, lf_c, C, n, m, tril, trilT):
    L = q.shape[0]
    b_c = jnp.sum(jnp.where(tril, lf_r, 0.0), axis=1, keepdims=True)
    b_r = jnp.sum(jnp.where(trilT, lf_c, 0.0), axis=0, keepdims=True)
    a_r = li_r - b_r
    a_c = li_c - b_c
    M_c = jnp.maximum(m, jnp.max(jnp.where(tril, a_r, -jnp.inf), axis=1, keepdims=True))
    w = jnp.exp(jnp.where(tril, a_r - M_c, -jnp.inf))
    dec = jnp.exp(m - M_c)
    ks = k * (D_HEAD ** -0.5)
    qb = q.astype(BF16)
    kb = ks.astype(BF16)
    vb = v.astype(BF16)
    s = lax.dot_general(qb, kb, NT, preferred_element_type=F32) * w
    numer = dec * jnp.dot(qb, C.astype(BF16), preferred_element_type=F32) \
        + jnp.dot(s.astype(BF16), vb, preferred_element_type=F32)
    dd = dec * jnp.sum(q * n, axis=1, keepdims=True) + jnp.sum(s, axis=1, keepdims=True)
    h = numer / jnp.maximum(jnp.abs(dd), jnp.exp(-(b_c + M_c)))
    b_last = b_c[L - 1:L, :]
    M_last = M_c[L - 1:L, :]
    m_new = b_last + M_last
    kw = ks * jnp.exp(a_c - M_last)
    dC = jnp.exp(m - M_last)
    C_new = dC * C + lax.dot_general(kw.astype(BF16), vb, TN, preferred_element_type=F32)
    n_new = dC * n + jnp.sum(kw, axis=0, keepdims=True)
    return h, C_new, n_new, m_new


def _mlstm_kernel(q_ref, k_ref, v_ref, o_ref, gc_ref, gr_ref, bc_ref, br_ref, gn_ref,
                  C0_ref, n0_ref, m0_ref, h_ref, C_ref, n_ref, m_ref, *, L, nsub):
    @pl.when(pl.program_id(1) == 0)
    def _():
        C_ref[...] = C0_ref[...]
        n_ref[...] = n0_ref[...]
        m_ref[...] = m0_ref[...]

    tril, _ = _masks(L)
    row = lax.broadcasted_iota(jnp.int32, (L, L), 0)
    col = lax.broadcasted_iota(jnp.int32, (L, L), 1)
    trilT = row <= col
    lane_c = lax.broadcasted_iota(jnp.int32, (L, N_GATE), 1)
    sub_r = lax.broadcasted_iota(jnp.int32, (N_GATE, L), 0)
    lane_m = lax.broadcasted_iota(jnp.int32, (1, N_HEADS), 1)

    def sub(j):
        r0 = _sub_row0(j, L)
        zc = gc_ref[0, j] + bc_ref[...]
        zr = gr_ref[0, j] + br_ref[...]
        gcv = jnp.where(lane_c < N_HEADS, zc, _log_sigmoid(zc))
        grv = jnp.where(sub_r < N_HEADS, zr, _log_sigmoid(zr))
        m_row = m_ref[j]
        m_out = m_row
        for h in range(N_HEADS):
            cs = slice(h * D_HEAD, (h + 1) * D_HEAD)
            q = q_ref[pl.ds(r0, L), cs].astype(F32)
            k = k_ref[pl.ds(r0, L), cs].astype(F32)
            v = v_ref[pl.ds(r0, L), cs].astype(F32)
            hh, C_new, n_new, m_new = _mlstm_head(
                q, k, v,
                grv[h:h + 1, :], grv[N_HEADS + h:N_HEADS + h + 1, :],
                gcv[:, h:h + 1], gcv[:, N_HEADS + h:N_HEADS + h + 1],
                C_ref[j, h], n_ref[j, h:h + 1, :], m_row[:, h:h + 1], tril, trilT)
            C_ref[j, h] = C_new
            n_ref[j, h:h + 1, :] = n_new
            m_out = jnp.where(lane_m == h, m_new, m_out)
            gate = jax.nn.sigmoid(o_ref[pl.ds(r0, L), cs].astype(F32))
            h_ref[pl.ds(r0, L), cs] = _head_norm_gate(hh, gn_ref[:, cs], gate).astype(h_ref.dtype)
        m_ref[j] = m_out

    if nsub == 1:
        sub(0)
    else:
        def body(j, carry):
            sub(j)
            return carry
        lax.fori_loop(0, nsub, body, 0)


def _scan_specs(nsub, L, n_chunks, col0):
    rows = nsub * L
    rb = lambda i, c: i * n_chunks + c
    col_specs = [pl.BlockSpec((rows, HALF), functools.partial(lambda i, c, j: (rb(i, c), j), j=col0 + j))
                 for j in range(4)]
    gate_specs = [pl.BlockSpec((1, nsub, L, N_GATE), lambda i, c: (rb(i, c), 0, 0, 0)),
                  pl.BlockSpec((1, nsub, N_GATE, L), lambda i, c: (rb(i, c), 0, 0, 0))]
    return rows, rb, col_specs, gate_specs


def _mlstm(proj, gc, gr, bias, gn, C0, n0, m0, L, nsub, act_dtype):
    B = C0.shape[0]
    n_tok = proj.shape[0]
    n_bblk = B // nsub
    n_chunks = n_tok // (n_bblk * nsub * L)
    rows, rb, col_specs, gate_specs = _scan_specs(nsub, L, n_chunks, 0)
    q_spec, k_spec, v_spec, o_spec = col_specs
    const2 = lambda i, c: (0, 0)
    st4 = pl.BlockSpec((nsub, N_HEADS, D_HEAD, D_HEAD), lambda i, c: (i, 0, 0, 0))
    st3 = pl.BlockSpec((nsub, N_HEADS, D_HEAD), lambda i, c: (i, 0, 0))
    stm = pl.BlockSpec((nsub, 1, N_HEADS), lambda i, c: (i, 0, 0))
    return pl.pallas_call(
        functools.partial(_mlstm_kernel, L=L, nsub=nsub),
        grid=(n_bblk, n_chunks),
        in_specs=[q_spec, k_spec, v_spec, o_spec] + gate_specs + [
            pl.BlockSpec((1, N_GATE), const2), pl.BlockSpec((N_GATE, 1), const2),
            pl.BlockSpec((1, HALF), const2), st4, st3, stm],
        out_specs=[pl.BlockSpec((rows, HALF), lambda i, c: (rb(i, c), 0)), st4, st3, stm],
        out_shape=[jax.ShapeDtypeStruct((n_tok, HALF), act_dtype),
                   jax.ShapeDtypeStruct(C0.shape, F32),
                   jax.ShapeDtypeStruct(n0.shape, F32),
                   jax.ShapeDtypeStruct(m0.shape, F32)],
        compiler_params=_params(("arbitrary", "arbitrary")),
        name="mlstm",
    )(proj, proj, proj, proj, gc, gr, bias.reshape(1, N_GATE), bias.reshape(N_GATE, 1),
      gn.reshape(1, HALF), C0, n0, m0)


def _conv_silu(u, prev, w_ref, cs, first_rows):
    L = u.shape[0]
    out = u * w_ref[CONV_W - 1:CONV_W, cs]
    for j in range(1, CONV_W):
        shifted = pltpu.roll(u, j, 0)
        head = jnp.where(first_rows < j, pltpu.roll(prev, j, 0), shifted[0:SUBLANES, :])
        if L > SUBLANES:
            shifted = jnp.concatenate([head, shifted[SUBLANES:, :]], axis=0)
        else:
            shifted = head
        out = out + shifted * w_ref[CONV_W - 1 - j:CONV_W - j, cs]
    return _silu(out)


def _l2n(x):
    return x * lax.rsqrt(jnp.sum(x * x, axis=-1, keepdims=True) + EPS)


def _gdn_head(q, k, v, g_r, g_c, beta_c, S, tril, trilT, strict):
    L = q.shape[0]
    q = _l2n(q) * (D_HEAD ** -0.5)
    k = _l2n(k)
    G_c = jnp.sum(jnp.where(tril, g_r, 0.0), axis=1, keepdims=True)
    G_r = jnp.sum(jnp.where(trilT, g_c, 0.0), axis=0, keepdims=True)
    decay = jnp.exp(jnp.where(tril, G_c - G_r, -jnp.inf))
    qb = q.astype(BF16)
    kb = k.astype(BF16)
    Sb = S.astype(BF16)
    kk = lax.dot_general(kb, kb, NT, preferred_element_type=F32)
    a = jnp.where(strict, beta_c * decay * kk, 0.0)
    gam = jnp.exp(G_c)
    rhs = beta_c * (v - gam * jnp.dot(kb, Sb, preferred_element_type=F32))
    u = rhs - _fdot(a, rhs)
    p = a
    span = 2
    while span < L:
        p = _fdot(p, p)
        u = u + _fdot(p, u)
        span *= 2
    ub = u.astype(BF16)
    qk = lax.dot_general(qb, kb, NT, preferred_element_type=F32) * decay
    o = gam * jnp.dot(qb, Sb, preferred_element_type=F32) + jnp.dot(qk.astype(BF16), ub, preferred_element_type=F32)
    G_last = G_c[L - 1:L, :]
    kd = k * jnp.exp(G_last - G_c)
    S_new = jnp.exp(G_last) * S + lax.dot_general(kd.astype(BF16), ub, TN, preferred_element_type=F32)
    return o, S_new


def _gdn_kernel(q_ref, k_ref, v_ref, z_ref, gc_ref, gr_ref, ac_ref, ar_ref, dc_ref, dr_ref,
                cw_ref, gn_ref, S0_ref, cv0_ref, h_ref, S_ref, cv_ref, qkv_scr, *, L, nsub):
    @pl.when(pl.program_id(1) == 0)
    def _():
        S_ref[...] = S0_ref[...]
        cv_ref[...] = cv0_ref[...]

    tril, strict = _masks(L)
    row = lax.broadcasted_iota(jnp.int32, (L, L), 0)
    col = lax.broadcasted_iota(jnp.int32, (L, L), 1)
    trilT = row <= col
    first_rows = lax.broadcasted_iota(jnp.int32, (SUBLANES, HALF), 0)

    def sub(j):
        r0 = _sub_row0(j, L)
        for p, ref in enumerate((q_ref, k_ref, v_ref)):
            cs = slice(p * HALF, (p + 1) * HALF)
            u = ref[pl.ds(r0, L), :].astype(F32)
            prev = cv_ref[j, :, cs]
            qkv_scr[p] = _conv_silu(u, prev, cw_ref, cs, first_rows)
            cv_ref[j, :, cs] = u[L - SUBLANES:, :]
        zc = gc_ref[0, j]
        zr = gr_ref[0, j]
        g_c = -jnp.exp(ac_ref[...]) * _softplus(zc[:, 0:N_HEADS] + dc_ref[...])
        g_r = -jnp.exp(ar_ref[...]) * _softplus(zr[0:N_HEADS, :] + dr_ref[...])
        beta_c = jax.nn.sigmoid(zc[:, N_HEADS:N_GATE])
        for h in range(N_HEADS):
            cs = slice(h * D_HEAD, (h + 1) * D_HEAD)
            o, S_new = _gdn_head(qkv_scr[0, :, cs], qkv_scr[1, :, cs], qkv_scr[2, :, cs],
                                 g_r[h:h + 1, :], g_c[:, h:h + 1], beta_c[:, h:h + 1],
                                 S_ref[j, h], tril, trilT, strict)
            S_ref[j, h] = S_new
            gate = _silu(z_ref[pl.ds(r0, L), cs].astype(F32))
            h_ref[pl.ds(r0, L), cs] = _head_norm_gate(o, gn_ref[:, cs], gate).astype(h_ref.dtype)

    if nsub == 1:
        sub(0)
    else:
        def body(j, carry):
            sub(j)
            return carry
        lax.fori_loop(0, nsub, body, 0)


def _gdn(proj, gc, gr, A_log, dt_bias, conv_w, gn, S0, cv0, L, nsub, act_dtype):
    B = S0.shape[0]
    n_tok = proj.shape[0]
    n_bblk = B // nsub
    n_chunks = n_tok // (n_bblk * nsub * L)
    rows, rb, col_specs, gate_specs = _scan_specs(nsub, L, n_chunks, 4)
    const2 = lambda i, c: (0, 0)
    st4 = pl.BlockSpec((nsub, N_HEADS, D_HEAD, D_HEAD), lambda i, c: (i, 0, 0, 0))
    stc = pl.BlockSpec((nsub, SUBLANES, C_CONV), lambda i, c: (i, 0, 0))
    return pl.pallas_call(
        functools.partial(_gdn_kernel, L=L, nsub=nsub),
        grid=(n_bblk, n_chunks),
        in_specs=col_specs + gate_specs + [
            pl.BlockSpec((1, N_HEADS), const2), pl.BlockSpec((N_HEADS, 1), const2),
            pl.BlockSpec((1, N_HEADS), const2), pl.BlockSpec((N_HEADS, 1), const2),
            pl.BlockSpec((CONV_W, C_CONV), const2), pl.BlockSpec((1, HALF), const2), st4, stc],
        out_specs=[pl.BlockSpec((rows, HALF), lambda i, c: (rb(i, c), 0)), st4, stc],
        out_shape=[jax.ShapeDtypeStruct((n_tok, HALF), act_dtype),
                   jax.ShapeDtypeStruct(S0.shape, F32),
                   jax.ShapeDtypeStruct(cv0.shape, F32)],
        scratch_shapes=[pltpu.VMEM((3, L, HALF), F32)],
        compiler_params=_params(("arbitrary", "arbitrary")),
        name="gdn",
    )(proj, proj, proj, proj, gc, gr,
      A_log.reshape(1, N_HEADS), A_log.reshape(N_HEADS, 1),
      dt_bias.reshape(1, N_HEADS), dt_bias.reshape(N_HEADS, 1),
      conv_w, gn.reshape(1, HALF), S0, cv0)


def _outproj_kernel(ha_ref, hb_ref, wa_ref, wb_ref, x_ref, g1_ref, sh_ref, sc_ref, gn_ref,
                    x1_ref, h2_ref, *, bb, tm):
    def body(i, carry):
        r0 = pl.multiple_of(i * ROW_CHUNK, ROW_CHUNK)
        mix = _bdot(ha_ref[pl.ds(r0, ROW_CHUNK), :], wa_ref[...]) \
            + _bdot(hb_ref[pl.ds(r0, ROW_CHUNK), :], wb_ref[...])
        x1 = _rows(x_ref, r0, ROW_CHUNK, bb) + _mod_rows(g1_ref, r0, ROW_CHUNK, bb) * mix
        _store_rows(x1_ref, r0, ROW_CHUNK, bb, x1)
        h2 = _rms(x1, gn_ref[...]) * (1.0 + _mod_rows(sc_ref, r0, ROW_CHUNK, bb)) \
            + _mod_rows(sh_ref, r0, ROW_CHUNK, bb)
        h2_ref[pl.ds(r0, ROW_CHUNK), :] = h2.astype(BF16)
        return carry
    lax.fori_loop(0, tm // ROW_CHUNK, body, 0)


def _outproj(ha, hb, w_out, x, g1, sh2, sc2, gn, bb, tt):
    B, T, D = x.shape
    tm = bb * tt
    x_map, mod_map = _tile_maps(T, tt)
    mod_spec = pl.BlockSpec((bb, 1, D), mod_map)
    return pl.pallas_call(
        functools.partial(_outproj_kernel, bb=bb, tm=tm),
        grid=((B * T) // tm,),
        in_specs=[pl.BlockSpec((tm, HALF), lambda m: (m, 0)),
                  pl.BlockSpec((tm, HALF), lambda m: (m, 0)),
                  pl.BlockSpec((HALF, D), lambda m: (0, 0)),
                  pl.BlockSpec((HALF, D), lambda m: (1, 0)),
                  pl.BlockSpec((bb, tt, D), x_map),
                  mod_spec, mod_spec, mod_spec,
                  pl.BlockSpec((1, D), lambda m: (0, 0))],
        out_specs=[pl.BlockSpec((bb, tt, D), x_map),
                   pl.BlockSpec((tm, D), lambda m: (m, 0))],
        out_shape=[jax.ShapeDtypeStruct((B, T, D), F32),
                   jax.ShapeDtypeStruct((B * T, D), BF16)],
        compiler_params=_params(("arbitrary",)),
        name="outproj",
    )(ha, hb, w_out, w_out, x, g1, sh2, sc2, gn)


def _mlp_kernel(h_ref, wu_ref, wd_ref, o_ref, wu_scr, wd_scr, *, tm):
    wu_scr[...] = wu_ref[...].astype(BF16)
    wd_scr[...] = wd_ref[...].astype(BF16)

    @pl.when(pl.program_id(1) == 0)
    def _():
        o_ref[...] = jnp.zeros_like(o_ref)

    def body(i, carry):
        r0 = pl.multiple_of(i * ROW_CHUNK, ROW_CHUNK)
        a = jnp.dot(h_ref[pl.ds(r0, ROW_CHUNK), :], wu_scr[...], preferred_element_type=F32)
        a = jnp.square(jnp.maximum(a, 0.0)).astype(BF16)
        o_ref[pl.ds(r0, ROW_CHUNK), :] += jnp.dot(a, wd_scr[...], preferred_element_type=F32)
        return carry
    lax.fori_loop(0, tm // ROW_CHUNK, body, 0)


def _mlp(h2, w_up, w_down, tm, tf=512):
    n_tok, D = h2.shape
    return pl.pallas_call(
        functools.partial(_mlp_kernel, tm=tm),
        grid=(n_tok // tm, D_FF // tf),
        in_specs=[pl.BlockSpec((tm, D), lambda m, f: (m, 0)),
                  pl.BlockSpec((D, tf), lambda m, f: (0, f)),
                  pl.BlockSpec((tf, D), lambda m, f: (f, 0))],
        out_specs=pl.BlockSpec((tm, D), lambda m, f: (m, 0)),
        out_shape=jax.ShapeDtypeStruct((n_tok, D), F32),
        scratch_shapes=[pltpu.VMEM((D, tf), BF16), pltpu.VMEM((tf, D), BF16)],
        compiler_params=_params(("arbitrary", "arbitrary")),
        name="mlp",
    )(h2, w_up, w_down)


def _final_kernel(x1_ref, y_ref, g2_ref, sh_ref, sc_ref, gn_ref, o_ref, *, bb, tm):
    def body(i, carry):
        r0 = pl.multiple_of(i * ROW_CHUNK, ROW_CHUNK)
        x2 = _rows(x1_ref, r0, ROW_CHUNK, bb) + _mod_rows(g2_ref, r0, ROW_CHUNK, bb) * y_ref[pl.ds(r0, ROW_CHUNK), :]
        out = _rms(x2, gn_ref[...]) * (1.0 + _mod_rows(sc_ref, r0, ROW_CHUNK, bb)) \
            + _mod_rows(sh_ref, r0, ROW_CHUNK, bb)
        _store_rows(o_ref, r0, ROW_CHUNK, bb, out)
        return carry
    lax.fori_loop(0, tm // ROW_CHUNK, body, 0)


def _final(x1, y, g2, shf, scf, gn, bb, tt):
    B, T, D = x1.shape
    tm = bb * tt
    x_map, mod_map = _tile_maps(T, tt)
    mod_spec = pl.BlockSpec((bb, 1, D), mod_map)
    return pl.pallas_call(
        functools.partial(_final_kernel, bb=bb, tm=tm),
        grid=((B * T) // tm,),
        in_specs=[pl.BlockSpec((bb, tt, D), x_map),
                  pl.BlockSpec((tm, D), lambda m: (m, 0)),
                  mod_spec, mod_spec, mod_spec,
                  pl.BlockSpec((1, D), lambda m: (0, 0))],
        out_specs=pl.BlockSpec((bb, tt, D), x_map),
        out_shape=jax.ShapeDtypeStruct((B, T, D), F32),
        compiler_params=_params(("arbitrary",)),
        name="final",
    )(x1, y, g2, shf, scf, gn)


def _gate_views(gates, off, n_blk, nsub, L):
    g = gates[:, off:off + N_GATE].reshape(n_blk, nsub, L, N_GATE)
    return g, jnp.swapaxes(g, 2, 3)


def _group_layer(x, mod, C0, n0, m0, S0, cv0, w, bb_big, tt_big, bb_small, tt_small, nsub, act_dtype):
    B, T, D = x.shape
    L = math.gcd(T, CHUNK)
    sh1, sc1, g1, sh2, sc2, g2 = [mod[:, i * D:(i + 1) * D].reshape(B, 1, D) for i in range(6)]
    proj, gates = _inproj(x, sh1, sc1, w["norm1"], w["w_in_main"], w["w_in_gate"], bb_big, tt_big, act_dtype)
    n_blk = (B * T) // (nsub * L)
    gca, gra = _gate_views(gates, 0, n_blk, nsub, L)
    gcb, grb = _gate_views(gates, N_GATE, n_blk, nsub, L)
    ha, C1, n1, m1 = _mlstm(proj, gca, gra, w["gate_b"], w["mlstm_g"], C0, n0,
                            m0.reshape(B, 1, N_HEADS), L, nsub, act_dtype)
    cv0p = jnp.concatenate([jnp.zeros((B, SUBLANES - (CONV_W - 1), C_CONV), F32), cv0.astype(F32)], axis=1)
    hb, S1, cv1 = _gdn(proj, gcb, grb, w["A_log"], w["dt_bias"], w["conv_w"], w["gdn_g"], S0, cv0p,
                       L, nsub, act_dtype)
    x1, h2 = _outproj(ha, hb, w["w_out"], x, g1, sh2, sc2, w["norm2"], bb_small, tt_small)
    y = _mlp(h2, w["w_up"], w["w_down"], bb_big * tt_big)
    return x1, y, g2, C1, n1, m1.reshape(B, N_HEADS), S1, cv1[:, SUBLANES - (CONV_W - 1):, :]


def _regroup_w_in(w_in):
    o = 0
    seg = {}
    for name, width in (("qa", HALF), ("ka", HALF), ("va", HALF), ("ia", N_HEADS), ("fa", N_HEADS), ("oa", HALF),
                        ("qb", HALF), ("kb", HALF), ("vb", HALF), ("ab", N_HEADS), ("bb", N_HEADS), ("zb", HALF)):
        seg[name] = w_in[:, o:o + width]
        o += width
    main = jnp.concatenate([seg[n] for n in ("qa", "ka", "va", "oa", "qb", "kb", "vb", "zb")], axis=1)
    gate = jnp.concatenate([seg[n] for n in ("ia", "fa", "ab", "bb")]
                           + [jnp.zeros((w_in.shape[0], LANES - 2 * N_GATE), w_in.dtype)], axis=1)
    return main.astype(BF16), gate.astype(BF16)


def kernel(x_prompt, x_sample, state_mlstm_C, state_mlstm_n, state_mlstm_m, state_gdn_S, state_gdn_conv,
           c_prompt, c_sample, ada_w, ada_b, norm1, w_in, mlstm_gate_bias, mlstm_norm, gdn_conv_w,
           gdn_A_log, gdn_dt_bias, gdn_norm, w_out, norm2, w_up, w_down, ada_final_w, ada_final_b,
           norm_final):
    Bp, Tp, D = x_prompt.shape
    Bs, Ts, _ = x_sample.shape
    depth = ada_w.shape[0]
    n_c = Bp + Bs
    pad = (-n_c) % SUBLANES
    c_all = jnp.concatenate([c_prompt, c_sample, jnp.zeros((pad, D), F32)], axis=0)
    mod_f = _ada(c_all, ada_final_w, ada_final_b)

    xp, xs = x_prompt, x_sample
    outs_p, outs_s = [], []
    for l in range(depth):
        main, gate = _regroup_w_in(w_in[l])
        w = dict(norm1=norm1[l].reshape(1, D), w_in_main=main, w_in_gate=gate, gate_b=mlstm_gate_bias[l],
                 mlstm_g=mlstm_norm[l], conv_w=gdn_conv_w[l], A_log=gdn_A_log[l], dt_bias=gdn_dt_bias[l],
                 gdn_g=gdn_norm[l], w_out=w_out[l].astype(BF16), norm2=norm2[l].reshape(1, D),
                 w_up=w_up[l], w_down=w_down[l])
        mod = _ada(c_all, ada_w[l], ada_b[l])
        zC = jnp.zeros((Bp, N_HEADS, D_HEAD, D_HEAD), F32)
        rp = _group_layer(xp, mod[:Bp], zC, jnp.zeros((Bp, N_HEADS, D_HEAD), F32), jnp.zeros((Bp, N_HEADS), F32),
                          zC, jnp.zeros((Bp, CONV_W - 1, C_CONV), F32), w,
                          1, 1024, 1, 512, 1, BF16)
        rs = _group_layer(xs, mod[Bp:n_c], state_mlstm_C[l], state_mlstm_n[l], state_mlstm_m[l],
                          state_gdn_S[l], state_gdn_conv[l], w,
                          Bs, Ts, 64, Ts, 8, F32)
        last = l == depth - 1
        nf = norm_final.reshape(1, D)
        res = []
        for (x1, y, g2, *st), sl, bb, tt in ((rp, slice(0, Bp), 1, 512), (rs, slice(Bp, n_c), 64, Ts)):
            B = x1.shape[0]
            if last:
                shf = mod_f[sl, :D].reshape(B, 1, D)
                scf = mod_f[sl, D:].reshape(B, 1, D)
                xo = _final(x1, y, g2, shf, scf, nf, bb, tt)
            else:
                xo = x1 + g2 * y.reshape(x1.shape)
            res.append((xo, st))
        (xp, st_p), (xs, st_s) = res
        outs_p.append(st_p)
        outs_s.append(st_s)

    def stack(outs, i):
        return jnp.stack([o[i] for o in outs])

    return (xp, xs,
            stack(outs_p, 0), stack(outs_p, 1), stack(outs_p, 2), stack(outs_p, 3), stack(outs_p, 4),
            stack(outs_s, 0), stack(outs_s, 1), stack(outs_s, 2), stack(outs_s, 3), stack(outs_s, 4))
```

```python
import functools
import math

import jax
import jax.numpy as jnp
from jax import lax
from jax.experimental import pallas as pl
from jax.experimental.pallas import tpu as pltpu

F32 = jnp.float32
BF16 = jnp.bfloat16

D_MODEL = 2048
N_HEADS = 8
D_HEAD = 128
HALF = N_HEADS * D_HEAD
CONV_W = 4
C_CONV = 3 * HALF
D_FF = 4 * D_MODEL
CHUNK = 64
EPS = 1e-6
N_GATE = 2 * N_HEADS

SUBLANES = 8
LANES = 128
VMEM_LIMIT = 56 * 1024 * 1024

NT = (((1,), (1,)), ((), ()))
TN = (((0,), (0,)), ((), ()))


def _params(sem):
    return pltpu.CompilerParams(dimension_semantics=sem, vmem_limit_bytes=VMEM_LIMIT)


def _silu(x):
    return x * jax.nn.sigmoid(x)


def _softplus(x):
    return jnp.maximum(x, 0.0) + jnp.log1p(jnp.exp(-jnp.abs(x)))


def _log_sigmoid(x):
    return jnp.minimum(x, 0.0) - jnp.log1p(jnp.exp(-jnp.abs(x)))


def _rms(x, g):
    return x * lax.rsqrt(jnp.mean(x * x, axis=-1, keepdims=True) + EPS) * g


def _bdot(a, b):
    return jnp.dot(a.astype(BF16), b.astype(BF16), preferred_element_type=F32)


def _split2(x):
    hi = x.astype(BF16)
    return hi, (x - hi.astype(F32)).astype(BF16)


def _dot3(a, b):
    ah, al = _split2(a)
    bh, bl = _split2(b)
    d = lambda x, y: jnp.dot(x, y, preferred_element_type=F32)
    return d(ah, bh) + (d(ah, bl) + d(al, bh))


def _prefix_dot(ones_l, x, ones_r):
    hi, lo = _split2(x)
    lo2 = (x - hi.astype(F32) - lo.astype(F32)).astype(BF16)
    if ones_r is None:
        c = ones_l.astype(BF16)
        d = lambda y: jnp.dot(c, y, preferred_element_type=F32)
    else:
        c = ones_r.astype(BF16)
        d = lambda y: jnp.dot(y, c, preferred_element_type=F32)
    return d(hi) + (d(lo) + d(lo2))


def _rows(ref, r0, rows, bb):
    if bb == 1:
        return ref[0, pl.ds(r0, rows), :]
    nb = rows // SUBLANES
    b0 = pl.multiple_of(r0 // SUBLANES, nb)
    return ref[pl.ds(b0, nb), :, :].reshape(rows, ref.shape[-1])


def _mod_rows(ref, r0, rows, bb):
    if bb == 1:
        return ref[0]
    nb = rows // SUBLANES
    b0 = pl.multiple_of(r0 // SUBLANES, nb)
    v = ref[pl.ds(b0, nb), :, :]
    return jnp.broadcast_to(v, (nb, SUBLANES, v.shape[-1])).reshape(rows, v.shape[-1])


def _store_rows(ref, r0, rows, bb, val):
    if bb == 1:
        ref[0, pl.ds(r0, rows), :] = val.astype(ref.dtype)
    else:
        nb = rows // SUBLANES
        b0 = pl.multiple_of(r0 // SUBLANES, nb)
        ref[pl.ds(b0, nb), :, :] = val.reshape(nb, SUBLANES, val.shape[-1]).astype(ref.dtype)


def _tile_maps(T, tt):
    tpb = T // tt
    x_map = lambda m, *_: (m // tpb, m % tpb, 0)
    mod_map = lambda m, *_: (m // tpb, 0, 0)
    return x_map, mod_map


ROW_CHUNK = 256


def _ada_kernel(c_ref, w_ref, b_ref, o_ref):
    s = _silu(c_ref[...])
    o_ref[...] = _bdot(s, w_ref[...]) + b_ref[...]


def _ada(c, w, b, tn=1024):
    M, K = c.shape
    N = w.shape[1]
    return pl.pallas_call(
        _ada_kernel,
        grid=(N // tn,),
        in_specs=[pl.BlockSpec((M, K), lambda n: (0, 0)),
                  pl.BlockSpec((K, tn), lambda n: (0, n)),
                  pl.BlockSpec((1, tn), lambda n: (0, n))],
        out_specs=pl.BlockSpec((M, tn), lambda n: (0, n)),
        out_shape=jax.ShapeDtypeStruct((M, N), F32),
        compiler_params=_params(("arbitrary",)),
        name="ada",
    )(c, w, b.reshape(1, N))


def _inproj_kernel(x_ref, sh_ref, sc_ref, g_ref, w_ref, wg_ref, o_ref, og_ref, h_scr, *, bb, tm):
    @pl.when(pl.program_id(1) == 0)
    def _():
        def body(i, carry):
            r0 = pl.multiple_of(i * ROW_CHUNK, ROW_CHUNK)
            x = _rows(x_ref, r0, ROW_CHUNK, bb)
            h = _rms(x, g_ref[...]) * (1.0 + _mod_rows(sc_ref, r0, ROW_CHUNK, bb)) \
                + _mod_rows(sh_ref, r0, ROW_CHUNK, bb)
            h_scr[pl.ds(r0, ROW_CHUNK), :] = h.astype(BF16)
            return carry
        lax.fori_loop(0, tm // ROW_CHUNK, body, 0)
        og_ref[...] = jnp.dot(h_scr[...], wg_ref[...], preferred_element_type=F32)

    o_ref[...] = jnp.dot(h_scr[...], w_ref[...], preferred_element_type=F32).astype(o_ref.dtype)


def _inproj(x, sh, sc, g, w, wg, bb, tt, out_dtype, tn=512):
    B, T, D = x.shape
    tm = bb * tt
    n_m = (B * T) // tm
    N = w.shape[1]
    x_map, mod_map = _tile_maps(T, tt)
    return pl.pallas_call(
        functools.partial(_inproj_kernel, bb=bb, tm=tm),
        grid=(n_m, N // tn),
        in_specs=[pl.BlockSpec((bb, tt, D), x_map),
                  pl.BlockSpec((bb, 1, D), mod_map),
                  pl.BlockSpec((bb, 1, D), mod_map),
                  pl.BlockSpec((1, D), lambda m, n: (0, 0)),
                  pl.BlockSpec((D, tn), lambda m, n: (0, n)),
                  pl.BlockSpec((D, LANES), lambda m, n: (0, 0))],
        out_specs=[pl.BlockSpec((tm, tn), lambda m, n: (m, n)),
                   pl.BlockSpec((tm, LANES), lambda m, n: (m, 0))],
        out_shape=[jax.ShapeDtypeStruct((B * T, N), out_dtype),
                   jax.ShapeDtypeStruct((B * T, LANES), F32)],
        scratch_shapes=[pltpu.VMEM((tm, D), BF16)],
        compiler_params=_params(("arbitrary", "arbitrary")),
        name="inproj",
    )(x, sh, sc, g, w, wg)


def _seq_row0(j, rows):
    return j * rows if isinstance(j, int) else pl.multiple_of(j * rows, rows)


def _tri(L):
    row = lax.broadcasted_iota(jnp.int32, (L, L), 0)
    col = lax.broadcasted_iota(jnp.int32, (L, L), 1)
    return col <= row, col < row, jnp.where(col <= row, 1.0, 0.0), jnp.where(row <= col, 1.0, 0.0)


def _head_norm_gate(hs, gs, gates):
    ms = [jnp.mean(h * h, axis=-1, keepdims=True) for h in hs]
    return [h * lax.rsqrt(v + EPS) * g * gate for h, v, g, gate in zip(hs, ms, gs, gates)]


def _for_each_seq(nsub, fn):
    if nsub == 1:
        fn(0)
    else:
        def body(j, carry):
            fn(j)
            return carry
        lax.fori_loop(0, nsub, body, 0)


def _scan_specs(nsub, nc, L, n_steps, col0):
    rows = nsub * nc * L
    rb = lambda i, c: i * n_steps + c
    col_specs = [pl.BlockSpec((rows, HALF), functools.partial(lambda i, c, j: (rb(i, c), j), j=col0 + j))
                 for j in range(4)]
    gate_specs = [pl.BlockSpec((1, nsub, nc * L, N_GATE), lambda i, c: (rb(i, c), 0, 0, 0)),
                  pl.BlockSpec((1, nsub, nc, N_GATE, L), lambda i, c: (rb(i, c), 0, 0, 0, 0))]
    return rows, rb, col_specs, gate_specs


def _mlstm_kernel(q_ref, k_ref, v_ref, o_ref, gc_ref, gr_ref, bc_ref, br_ref, gn_ref,
                  C0_ref, n0_ref, m0_ref, h_ref, C_ref, n_ref, m_ref, *, L, nc, nsub):
    @pl.when(pl.program_id(1) == 0)
    def _():
        C_ref[...] = C0_ref[...]
        n_ref[...] = n0_ref[...]
        m_ref[...] = m0_ref[...]

    tril, _, tril_f, triu_f = _tri(L)
    lane_m = lax.broadcasted_iota(jnp.int32, (1, N_HEADS), 1)
    H = range(N_HEADS)
    hs = [slice(h * D_HEAD, (h + 1) * D_HEAD) for h in H]

    def seq(j):
        m_row = m_ref[j]
        for c in range(nc):
            rows = pl.ds(_seq_row0(j, nc * L) + c * L, L)
            zc = gc_ref[0, j, c * L:(c + 1) * L, :] + bc_ref[...]
            zr = gr_ref[0, j, c] + br_ref[...]
            li_c = zc[:, 0:N_HEADS]
            lf_c = _log_sigmoid(zc[:, N_HEADS:N_GATE])
            li_r = zr[0:N_HEADS, :]
            lf_r = _log_sigmoid(zr[N_HEADS:N_GATE, :])
            b_c = _prefix_dot(tril_f, lf_c, None)
            b_r = _prefix_dot(None, lf_r, triu_f)
            a_c = li_c - b_c
            a_r = li_r - b_r
            q = [q_ref[rows, hs[h]].astype(F32) for h in H]
            ks = [k_ref[rows, hs[h]].astype(F32) * (D_HEAD ** -0.5) for h in H]
            qb = [x.astype(BF16) for x in q]
            kb = [x.astype(BF16) for x in ks]
            vb = [v_ref[rows, hs[h]].astype(BF16) for h in H]
            qk = [lax.dot_general(qb[h], kb[h], NT, preferred_element_type=F32) for h in H]
            A_c = [jnp.max(jnp.where(tril, a_r[h:h + 1, :], -jnp.inf), axis=1, keepdims=True) for h in H]
            p0 = [qk[h] * jnp.exp(jnp.where(tril, a_r[h:h + 1, :] - A_c[h], -jnp.inf)) for h in H]
            rs = [jnp.sum(p0[h], axis=1, keepdims=True) for h in H]
            sv = [jnp.dot(p0[h].astype(BF16), vb[h], preferred_element_type=F32) for h in H]
            A_last = [A_c[h][L - 1:L, :] for h in H]
            kw = [ks[h] * jnp.exp(a_c[:, h:h + 1] - A_last[h]) for h in H]
            kv = [lax.dot_general(kw[h].astype(BF16), vb[h], TN, preferred_element_type=F32) for h in H]
            ksum = [jnp.sum(kw[h], axis=0, keepdims=True) for h in H]
            C = [C_ref[j, h] for h in H]
            n = [n_ref[j, h:h + 1, :] for h in H]
            m = [m_row[:, h:h + 1] for h in H]
            qC = [jnp.dot(qb[h], C[h].astype(BF16), preferred_element_type=F32) for h in H]
            M_c = [jnp.maximum(m[h], A_c[h]) for h in H]
            r = [jnp.exp(A_c[h] - M_c[h]) for h in H]
            dec = [jnp.exp(m[h] - M_c[h]) for h in H]
            qn = [jnp.sum(q[h] * n[h], axis=1, keepdims=True) for h in H]
            numer = [dec[h] * qC[h] + r[h] * sv[h] for h in H]
            dd = [dec[h] * qn[h] + r[h] * rs[h] for h in H]
            floor = [jnp.exp(-(b_c[:, h:h + 1] + M_c[h])) for h in H]
            hh = [numer[h] / jnp.maximum(jnp.abs(dd[h]), floor[h]) for h in H]
            M_last = [M_c[h][L - 1:L, :] for h in H]
            rl = [jnp.exp(A_last[h] - M_last[h]) for h in H]
            dC = [jnp.exp(m[h] - M_last[h]) for h in H]
            gate = [jax.nn.sigmoid(o_ref[rows, hs[h]].astype(F32)) for h in H]
            out = _head_norm_gate(hh, [gn_ref[:, hs[h]] for h in H], gate)
            for h in H:
                C_ref[j, h] = dC[h] * C[h] + rl[h] * kv[h]
                n_ref[j, h:h + 1, :] = dC[h] * n[h] + rl[h] * ksum[h]
                m_row = jnp.where(lane_m == h, b_c[L - 1:L, h:h + 1] + M_last[h], m_row)
                h_ref[rows, hs[h]] = out[h].astype(h_ref.dtype)
        m_ref[j] = m_row

    _for_each_seq(nsub, seq)


def _mlstm(proj, gc, gr, bias, gn, C0, n0, m0, L, nc, nsub, act_dtype):
    B = C0.shape[0]
    n_tok = proj.shape[0]
    n_bblk = B // nsub
    n_steps = n_tok // (n_bblk * nsub * nc * L)
    rows, rb, col_specs, gate_specs = _scan_specs(nsub, nc, L, n_steps, 0)
    const2 = lambda i, c: (0, 0)
    st4 = pl.BlockSpec((nsub, N_HEADS, D_HEAD, D_HEAD), lambda i, c: (i, 0, 0, 0))
    st3 = pl.BlockSpec((nsub, N_HEADS, D_HEAD), lambda i, c: (i, 0, 0))
    stm = pl.BlockSpec((nsub, 1, N_HEADS), lambda i, c: (i, 0, 0))
    return pl.pallas_call(
        functools.partial(_mlstm_kernel, L=L, nc=nc, nsub=nsub),
        grid=(n_bblk, n_steps),
        in_specs=col_specs + gate_specs + [
            pl.BlockSpec((1, N_GATE), const2), pl.BlockSpec((N_GATE, 1), const2),
            pl.BlockSpec((1, HALF), const2), st4, st3, stm],
        out_specs=[pl.BlockSpec((rows, HALF), lambda i, c: (rb(i, c), 0)), st4, st3, stm],
        out_shape=[jax.ShapeDtypeStruct((n_tok, HALF), act_dtype),
                   jax.ShapeDtypeStruct(C0.shape, F32),
                   jax.ShapeDtypeStruct(n0.shape, F32),
                   jax.ShapeDtypeStruct(m0.shape, F32)],
        compiler_params=_params(("arbitrary", "arbitrary")),
        name="mlstm",
    )(proj, proj, proj, proj, gc, gr, bias.reshape(1, N_GATE), bias.reshape(N_GATE, 1),
      gn.reshape(1, HALF), C0, n0, m0)


def _conv_silu(u, prev, w_ref, cs, first_rows):
    R = u.shape[0]
    out = u * w_ref[CONV_W - 1:CONV_W, cs]
    for j in range(1, CONV_W):
        shifted = pltpu.roll(u, j, 0)
        head = jnp.where(first_rows < j, pltpu.roll(prev, j, 0), shifted[0:SUBLANES, :])
        if R > SUBLANES:
            shifted = jnp.concatenate([head, shifted[SUBLANES:, :]], axis=0)
        else:
            shifted = head
        out = out + shifted * w_ref[CONV_W - 1 - j:CONV_W - j, cs]
    return _silu(out)


def _l2n(x):
    return x * lax.rsqrt(jnp.sum(x * x, axis=-1, keepdims=True) + EPS)


def _unit_lower_inverse(mats):
    L = mats[0].shape[0]
    B8 = SUBLANES
    n = range(len(mats))
    lane = lax.broadcasted_iota(jnp.int32, (B8, L), 1)
    sub = lax.broadcasted_iota(jnp.int32, (B8, L), 0)
    blk = lane // B8
    packed = []
    for a in mats:
        p = a[0:B8, :]
        for b in range(1, L // B8):
            p = jnp.where(blk == b, a[b * B8:(b + 1) * B8, :], p)
        packed.append(p)
    eye8 = jnp.where(lane - blk * B8 == sub, 1.0, 0.0)
    x = [eye8 for _ in n]
    for s in range(B8 - 1):
        cols = [jnp.take_along_axis(packed[i], blk * B8 + s, axis=1) for i in n]
        x = [x[i] - cols[i] * x[i][s:s + 1, :] for i in n]
    if L == B8:
        return x
    d = [jnp.concatenate([jnp.where(blk == b, x[i], 0.0) for b in range(L // B8)], axis=0) for i in n]
    row = lax.broadcasted_iota(jnp.int32, (L, L), 0)
    colm = lax.broadcasted_iota(jnp.int32, (L, L), 1)
    size = B8
    while size < L:
        rb, cb = row // size, colm // size
        join = (rb // 2 == cb // 2) & (rb != cb)
        ad = [_dot3(jnp.where(join, mats[i], 0.0), d[i]) for i in n]
        dad = [_dot3(d[i], ad[i]) for i in n]
        d = [d[i] - dad[i] for i in n]
        size *= 2
    return d


def _gdn_kernel(q_ref, k_ref, v_ref, z_ref, gc_ref, gr_ref, ac_ref, ar_ref, dc_ref, dr_ref,
                cw_ref, gn_ref, S0_ref, cv0_ref, h_ref, S_ref, cv_ref,
                qkv_scr, wq_scr, u_scr, qk_scr, kd_scr, gs_scr, *, L, nc, nsub):
    @pl.when(pl.program_id(1) == 0)
    def _():
        S_ref[...] = S0_ref[...]
        cv_ref[...] = cv0_ref[...]

    R = nc * L
    tril, strict, tril_f, triu_f = _tri(L)
    first_rows = lax.broadcasted_iota(jnp.int32, (SUBLANES, HALF), 0)
    H = range(N_HEADS)
    hs = [slice(h * D_HEAD, (h + 1) * D_HEAD) for h in H]

    def seq(j):
        r0 = _seq_row0(j, R)
        for p, ref in enumerate((q_ref, k_ref, v_ref)):
            cs = slice(p * HALF, (p + 1) * HALF)
            u = ref[pl.ds(r0, R), :].astype(F32)
            prev = cv_ref[j, :, cs]
            qkv_scr[p] = _conv_silu(u, prev, cw_ref, cs, first_rows)
            cv_ref[j, :, cs] = u[R - SUBLANES:, :]
        for c in range(nc):
            rc = slice(c * L, (c + 1) * L)
            zc = gc_ref[0, j, rc, :]
            zr = gr_ref[0, j, c]
            g_c = -jnp.exp(ac_ref[...]) * _softplus(zc[:, 0:N_HEADS] + dc_ref[...])
            g_r = -jnp.exp(ar_ref[...]) * _softplus(zr[0:N_HEADS, :] + dr_ref[...])
            beta = jax.nn.sigmoid(zc[:, N_HEADS:N_GATE])
            G_c = _prefix_dot(tril_f, g_c, None)
            G_r = _prefix_dot(None, g_r, triu_f)
            gam = jnp.exp(G_c)
            G_last = G_c[L - 1:L, :]
            gs_scr[c] = jnp.exp(G_last)
            kdec = jnp.exp(G_last - G_c)
            bg = beta * gam
            qn = [_l2n(qkv_scr[0, rc, hs[h]]) * (D_HEAD ** -0.5) for h in H]
            kn = [_l2n(qkv_scr[1, rc, hs[h]]) for h in H]
            kb = [x.astype(BF16) for x in kn]
            kq = [lax.dot_general(jnp.concatenate([kn[h], qn[h]], axis=0).astype(BF16), kb[h], NT,
                                  preferred_element_type=F32) for h in H]
            decay = [jnp.exp(jnp.where(tril, G_c[:, h:h + 1] - G_r[h:h + 1, :], -jnp.inf)) for h in H]
            a = [jnp.where(strict, beta[:, h:h + 1] * decay[h] * kq[h][0:L, :], 0.0) for h in H]
            for h in H:
                qk_scr[c, h] = kq[h][L:2 * L, :] * decay[h]
                kd_scr[c, h] = kn[h] * kdec[:, h:h + 1]
            T = _unit_lower_inverse(a)
            rhs = [jnp.concatenate([bg[:, h:h + 1] * kn[h], beta[:, h:h + 1] * qkv_scr[2, rc, hs[h]]], axis=1)
                   for h in H]
            wu = [_dot3(T[h], rhs[h]) for h in H]
            for h in H:
                wq_scr[c, h] = jnp.concatenate([wu[h][:, 0:D_HEAD], gam[:, h:h + 1] * qn[h]], axis=0).astype(BF16)
                u_scr[c, h] = wu[h][:, D_HEAD:]
        for c in range(nc):
            rows = pl.ds(r0 + c * L, L)
            S = [S_ref[j, h] for h in H]
            gS = gs_scr[c]
            P = [jnp.dot(wq_scr[c, h], S[h].astype(BF16), preferred_element_type=F32) for h in H]
            ub = [(u_scr[c, h] - P[h][0:L, :]).astype(BF16) for h in H]
            o = [P[h][L:2 * L, :] + jnp.dot(qk_scr[c, h].astype(BF16), ub[h], preferred_element_type=F32) for h in H]
            dS = [lax.dot_general(kd_scr[c, h].astype(BF16), ub[h], TN, preferred_element_type=F32) for h in H]
            gate = [_silu(z_ref[rows, hs[h]].astype(F32)) for h in H]
            out = _head_norm_gate(o, [gn_ref[:, hs[h]] for h in H], gate)
            for h in H:
                S_ref[j, h] = gS[:, h:h + 1] * S[h] + dS[h]
                h_ref[rows, hs[h]] = out[h].astype(h_ref.dtype)

    _for_each_seq(nsub, seq)


def _gdn(proj, gc, gr, A_log, dt_bias, conv_w, gn, S0, cv0, L, nc, nsub, act_dtype):
    B = S0.shape[0]
    n_tok = proj.shape[0]
    n_bblk = B // nsub
    n_steps = n_tok // (n_bblk * nsub * nc * L)
    rows, rb, col_specs, gate_specs = _scan_specs(nsub, nc, L, n_steps, 4)
    const2 = lambda i, c: (0, 0)
    st4 = pl.BlockSpec((nsub, N_HEADS, D_HEAD, D_HEAD), lambda i, c: (i, 0, 0, 0))
    stc = pl.BlockSpec((nsub, SUBLANES, C_CONV), lambda i, c: (i, 0, 0))
    return pl.pallas_call(
        functools.partial(_gdn_kernel, L=L, nc=nc, nsub=nsub),
        grid=(n_bblk, n_steps),
        in_specs=col_specs + gate_specs + [
            pl.BlockSpec((1, N_HEADS), const2), pl.BlockSpec((N_HEADS, 1), const2),
            pl.BlockSpec((1, N_HEADS), const2), pl.BlockSpec((N_HEADS, 1), const2),
            pl.BlockSpec((CONV_W, C_CONV), const2), pl.BlockSpec((1, HALF), const2), st4, stc],
        out_specs=[pl.BlockSpec((rows, HALF), lambda i, c: (rb(i, c), 0)), st4, stc],
        out_shape=[jax.ShapeDtypeStruct((n_tok, HALF), act_dtype),
                   jax.ShapeDtypeStruct(S0.shape, F32),
                   jax.ShapeDtypeStruct(cv0.shape, F32)],
        scratch_shapes=[pltpu.VMEM((3, nc * L, HALF), F32),
                        pltpu.VMEM((nc, N_HEADS, 2 * L, D_HEAD), BF16),
                        pltpu.VMEM((nc, N_HEADS, L, D_HEAD), F32),
                        pltpu.VMEM((nc, N_HEADS, L, L), F32),
                        pltpu.VMEM((nc, N_HEADS, L, D_HEAD), F32),
                        pltpu.VMEM((nc, 1, N_HEADS), F32)],
        compiler_params=_params(("arbitrary", "arbitrary")),
        name="gdn",
    )(proj, proj, proj, proj, gc, gr,
      A_log.reshape(1, N_HEADS), A_log.reshape(N_HEADS, 1),
      dt_bias.reshape(1, N_HEADS), dt_bias.reshape(N_HEADS, 1),
      conv_w, gn.reshape(1, HALF), S0, cv0)


def _outproj_kernel(ha_ref, hb_ref, wa_ref, wb_ref, x_ref, g1_ref, sh_ref, sc_ref, gn_ref,
                    x1_ref, h2_ref, *, bb, tm):
    def body(i, carry):
        r0 = pl.multiple_of(i * ROW_CHUNK, ROW_CHUNK)
        mix = _bdot(ha_ref[pl.ds(r0, ROW_CHUNK), :], wa_ref[...]) \
            + _bdot(hb_ref[pl.ds(r0, ROW_CHUNK), :], wb_ref[...])
        x1 = _rows(x_ref, r0, ROW_CHUNK, bb) + _mod_rows(g1_ref, r0, ROW_CHUNK, bb) * mix
        _store_rows(x1_ref, r0, ROW_CHUNK, bb, x1)
        h2 = _rms(x1, gn_ref[...]) * (1.0 + _mod_rows(sc_ref, r0, ROW_CHUNK, bb)) \
            + _mod_rows(sh_ref, r0, ROW_CHUNK, bb)
        h2_ref[pl.ds(r0, ROW_CHUNK), :] = h2.astype(BF16)
        return carry
    lax.fori_loop(0, tm // ROW_CHUNK, body, 0)


def _outproj(ha, hb, w_out, x, g1, sh2, sc2, gn, bb, tt):
    B, T, D = x.shape
    tm = bb * tt
    x_map, mod_map = _tile_maps(T, tt)
    mod_spec = pl.BlockSpec((bb, 1, D), mod_map)
    return pl.pallas_call(
        functools.partial(_outproj_kernel, bb=bb, tm=tm),
        grid=((B * T) // tm,),
        in_specs=[pl.BlockSpec((tm, HALF), lambda m: (m, 0)),
                  pl.BlockSpec((tm, HALF), lambda m: (m, 0)),
                  pl.BlockSpec((HALF, D), lambda m: (0, 0)),
                  pl.BlockSpec((HALF, D), lambda m: (1, 0)),
                  pl.BlockSpec((bb, tt, D), x_map),
                  mod_spec, mod_spec, mod_spec,
                  pl.BlockSpec((1, D), lambda m: (0, 0))],
        out_specs=[pl.BlockSpec((bb, tt, D), x_map),
                   pl.BlockSpec((tm, D), lambda m: (m, 0))],
        out_shape=[jax.ShapeDtypeStruct((B, T, D), F32),
                   jax.ShapeDtypeStruct((B * T, D), BF16)],
        compiler_params=_params(("arbitrary",)),
        name="outproj",
    )(ha, hb, w_out, w_out, x, g1, sh2, sc2, gn)


def _mlp_kernel(h_ref, wu_ref, wd_ref, o_ref, wu_scr, wd_scr, *, tm):
    wu_scr[...] = wu_ref[...].astype(BF16)
    wd_scr[...] = wd_ref[...].astype(BF16)

    @pl.when(pl.program_id(1) == 0)
    def _():
        o_ref[...] = jnp.zeros_like(o_ref)

    def body(i, carry):
        r0 = pl.multiple_of(i * ROW_CHUNK, ROW_CHUNK)
        a = jnp.dot(h_ref[pl.ds(r0, ROW_CHUNK), :], wu_scr[...], preferred_element_type=F32)
        a = jnp.square(jnp.maximum(a, 0.0)).astype(BF16)
        o_ref[pl.ds(r0, ROW_CHUNK), :] += jnp.dot(a, wd_scr[...], preferred_element_type=F32)
        return carry
    lax.fori_loop(0, tm // ROW_CHUNK, body, 0)


def _mlp(h2, w_up, w_down, tm, tf=512):
    n_tok, D = h2.shape
    return pl.pallas_call(
        functools.partial(_mlp_kernel, tm=tm),
        grid=(n_tok // tm, D_FF // tf),
        in_specs=[pl.BlockSpec((tm, D), lambda m, f: (m, 0)),
                  pl.BlockSpec((D, tf), lambda m, f: (0, f)),
                  pl.BlockSpec((tf, D), lambda m, f: (f, 0))],
        out_specs=pl.BlockSpec((tm, D), lambda m, f: (m, 0)),
        out_shape=jax.ShapeDtypeStruct((n_tok, D), F32),
        scratch_shapes=[pltpu.VMEM((D, tf), BF16), pltpu.VMEM((tf, D), BF16)],
        compiler_params=_params(("arbitrary", "arbitrary")),
        name="mlp",
    )(h2, w_up, w_down)


def _final_kernel(x1_ref, y_ref, g2_ref, sh_ref, sc_ref, gn_ref, o_ref, *, bb, tm):
    def body(i, carry):
        r0 = pl.multiple_of(i * ROW_CHUNK, ROW_CHUNK)
        x2 = _rows(x1_ref, r0, ROW_CHUNK, bb) + _mod_rows(g2_ref, r0, ROW_CHUNK, bb) * y_ref[pl.ds(r0, ROW_CHUNK), :]
        out = _rms(x2, gn_ref[...]) * (1.0 + _mod_rows(sc_ref, r0, ROW_CHUNK, bb)) \
            + _mod_rows(sh_ref, r0, ROW_CHUNK, bb)
        _store_rows(o_ref, r0, ROW_CHUNK, bb, out)
        return carry
    lax.fori_loop(0, tm // ROW_CHUNK, body, 0)


def _final(x1, y, g2, shf, scf, gn, bb, tt):
    B, T, D = x1.shape
    tm = bb * tt
    x_map, mod_map = _tile_maps(T, tt)
    mod_spec = pl.BlockSpec((bb, 1, D), mod_map)
    return pl.pallas_call(
        functools.partial(_final_kernel, bb=bb, tm=tm),
        grid=((B * T) // tm,),
        in_specs=[pl.BlockSpec((bb, tt, D), x_map),
                  pl.BlockSpec((tm, D), lambda m: (m, 0)),
                  mod_spec, mod_spec, mod_spec,
                  pl.BlockSpec((1, D), lambda m: (0, 0))],
        out_specs=pl.BlockSpec((bb, tt, D), x_map),
        out_shape=jax.ShapeDtypeStruct((B, T, D), F32),
        compiler_params=_params(("arbitrary",)),
        name="final",
    )(x1, y, g2, shf, scf, gn)


def _gate_views(gates, off, n_blk, nsub, nc, L):
    g = gates[:, off:off + N_GATE].reshape(n_blk, nsub, nc, L, N_GATE)
    return g.reshape(n_blk, nsub, nc * L, N_GATE), jnp.swapaxes(g, 3, 4)


def _group_layer(x, mod, C0, n0, m0, S0, cv0, w, bb_big, tt_big, bb_small, tt_small, nc, nsub, act_dtype):
    B, T, D = x.shape
    L = math.gcd(T, CHUNK)
    sh1, sc1, g1, sh2, sc2, g2 = [mod[:, i * D:(i + 1) * D].reshape(B, 1, D) for i in range(6)]
    proj, gates = _inproj(x, sh1, sc1, w["norm1"], w["w_in_main"], w["w_in_gate"], bb_big, tt_big, act_dtype)
    n_blk = (B * T) // (nsub * nc * L)
    gca, gra = _gate_views(gates, 0, n_blk, nsub, nc, L)
    gcb, grb = _gate_views(gates, N_GATE, n_blk, nsub, nc, L)
    ha, C1, n1, m1 = _mlstm(proj, gca, gra, w["gate_b"], w["mlstm_g"], C0, n0,
                            m0.reshape(B, 1, N_HEADS), L, nc, nsub, act_dtype)
    cv0p = jnp.concatenate([jnp.zeros((B, SUBLANES - (CONV_W - 1), C_CONV), F32), cv0.astype(F32)], axis=1)
    hb, S1, cv1 = _gdn(proj, gcb, grb, w["A_log"], w["dt_bias"], w["conv_w"], w["gdn_g"], S0, cv0p,
                       L, nc, nsub, act_dtype)
    x1, h2 = _outproj(ha, hb, w["w_out"], x, g1, sh2, sc2, w["norm2"], bb_small, tt_small)
    y = _mlp(h2, w["w_up"], w["w_down"], bb_big * tt_big)
    return x1, y, g2, C1, n1, m1.reshape(B, N_HEADS), S1, cv1[:, SUBLANES - (CONV_W - 1):, :]


def _regroup_w_in(w_in):
    o = 0
    seg = {}
    for name, width in (("qa", HALF), ("ka", HALF), ("va", HALF), ("ia", N_HEADS), ("fa", N_HEADS), ("oa", HALF),
                        ("qb", HALF), ("kb", HALF), ("vb", HALF), ("ab", N_HEADS), ("bb", N_HEADS), ("zb", HALF)):
        seg[name] = w_in[:, o:o + width]
        o += width
    main = jnp.concatenate([seg[n] for n in ("qa", "ka", "va", "oa", "qb", "kb", "vb", "zb")], axis=1)
    gate = jnp.concatenate([seg[n] for n in ("ia", "fa", "ab", "bb")]
                           + [jnp.zeros((w_in.shape[0], LANES - 2 * N_GATE), w_in.dtype)], axis=1)
    return main.astype(BF16), gate.astype(BF16)


def kernel(x_prompt, x_sample, state_mlstm_C, state_mlstm_n, state_mlstm_m, state_gdn_S, state_gdn_conv,
           c_prompt, c_sample, ada_w, ada_b, norm1, w_in, mlstm_gate_bias, mlstm_norm, gdn_conv_w,
           gdn_A_log, gdn_dt_bias, gdn_norm, w_out, norm2, w_up, w_down, ada_final_w, ada_final_b,
           norm_final):
    Bp, Tp, D = x_prompt.shape
    Bs, Ts, _ = x_sample.shape
    depth = ada_w.shape[0]
    n_c = Bp + Bs
    pad = (-n_c) % SUBLANES
    c_all = jnp.concatenate([c_prompt, c_sample, jnp.zeros((pad, D), F32)], axis=0)
    mod_f = _ada(c_all, ada_final_w, ada_final_b)

    xp, xs = x_prompt, x_sample
    outs_p, outs_s = [], []
    for l in range(depth):
        main, gate = _regroup_w_in(w_in[l])
        w = dict(norm1=norm1[l].reshape(1, D), w_in_main=main, w_in_gate=gate, gate_b=mlstm_gate_bias[l],
                 mlstm_g=mlstm_norm[l], conv_w=gdn_conv_w[l], A_log=gdn_A_log[l], dt_bias=gdn_dt_bias[l],
                 gdn_g=gdn_norm[l], w_out=w_out[l].astype(BF16), norm2=norm2[l].reshape(1, D),
                 w_up=w_up[l], w_down=w_down[l])
        mod = _ada(c_all, ada_w[l], ada_b[l])
        zC = jnp.zeros((Bp, N_HEADS, D_HEAD, D_HEAD), F32)
        rp = _group_layer(xp, mod[:Bp], zC, jnp.zeros((Bp, N_HEADS, D_HEAD), F32), jnp.zeros((Bp, N_HEADS), F32),
                          zC, jnp.zeros((Bp, CONV_W - 1, C_CONV), F32), w,
                          1, 1024, 1, 512, 2, 1, BF16)
        rs = _group_layer(xs, mod[Bp:n_c], state_mlstm_C[l], state_mlstm_n[l], state_mlstm_m[l],
                          state_gdn_S[l], state_gdn_conv[l], w,
                          Bs, Ts, 64, Ts, 1, 8, F32)
        last = l == depth - 1
        nf = norm_final.reshape(1, D)
        res = []
        for (x1, y, g2, *st), sl, bb, tt in ((rp, slice(0, Bp), 1, 512), (rs, slice(Bp, n_c), 64, Ts)):
            B = x1.shape[0]
            if last:
                shf = mod_f[sl, :D].reshape(B, 1, D)
                scf = mod_f[sl, D:].reshape(B, 1, D)
                xo = _final(x1, y, g2, shf, scf, nf, bb, tt)
            else:
                xo = x1 + g2 * y.reshape(x1.shape)
            res.append((xo, st))
        (xp, st_p), (xs, st_s) = res
        outs_p.append(st_p)
        outs_s.append(st_s)

    def stack(outs, i):
        return jnp.stack([o[i] for o in outs])

    return (xp, xs,
            stack(outs_p, 0), stack(outs_p, 1), stack(outs_p, 2), stack(outs_p, 3), stack(outs_p, 4),
            stack(outs_s, 0), stack(outs_s, 1), stack(outs_s, 2), stack(outs_s, 3), stack(outs_s, 4))
```

```python
import functools
import math

import jax
import jax.numpy as jnp
from jax import lax
from jax.experimental import pallas as pl
from jax.experimental.pallas import tpu as pltpu

F32 = jnp.float32
BF16 = jnp.bfloat16

D_MODEL = 2048
N_HEADS = 8
D_HEAD = 128
HALF = N_HEADS * D_HEAD
CONV_W = 4
C_CONV = 3 * HALF
D_FF = 4 * D_MODEL
CHUNK = 64
EPS = 1e-6
N_GATE = 2 * N_HEADS

SUBLANES = 8
LANES = 128
VMEM_LIMIT = 56 * 1024 * 1024

NT = (((1,), (1,)), ((), ()))
TN = (((0,), (0,)), ((), ()))


def _params(sem):
    return pltpu.CompilerParams(dimension_semantics=sem, vmem_limit_bytes=VMEM_LIMIT)


def _silu(x):
    return x * jax.nn.sigmoid(x)


def _softplus(x):
    return jnp.maximum(x, 0.0) + jnp.log1p(jnp.exp(-jnp.abs(x)))


def _log_sigmoid(x):
    return jnp.minimum(x, 0.0) - jnp.log1p(jnp.exp(-jnp.abs(x)))


def _rms(x, g):
    return x * lax.rsqrt(jnp.mean(x * x, axis=-1, keepdims=True) + EPS) * g


def _bdot(a, b):
    return jnp.dot(a.astype(BF16), b.astype(BF16), preferred_element_type=F32)


def _split2(x):
    hi = x.astype(BF16)
    return hi, (x - hi.astype(F32)).astype(BF16)


def _dot3(a, b):
    ah, al = _split2(a)
    bh, bl = _split2(b)
    d = lambda x, y: jnp.dot(x, y, preferred_element_type=F32)
    return d(ah, bh) + (d(ah, bl) + d(al, bh))


def _prefix_dot(ones_l, x, ones_r):
    hi, lo = _split2(x)
    lo2 = (x - hi.astype(F32) - lo.astype(F32)).astype(BF16)
    if ones_r is None:
        c = ones_l.astype(BF16)
        d = lambda y: jnp.dot(c, y, preferred_element_type=F32)
    else:
        c = ones_r.astype(BF16)
        d = lambda y: jnp.dot(y, c, preferred_element_type=F32)
    return d(hi) + (d(lo) + d(lo2))


def _rows(ref, r0, rows, bb):
    if bb == 1:
        return ref[0, pl.ds(r0, rows), :]
    nb = rows // SUBLANES
    b0 = pl.multiple_of(r0 // SUBLANES, nb)
    return ref[pl.ds(b0, nb), :, :].reshape(rows, ref.shape[-1])


def _mod_rows(ref, r0, rows, bb):
    if bb == 1:
        return ref[0]
    nb = rows // SUBLANES
    b0 = pl.multiple_of(r0 // SUBLANES, nb)
    v = ref[pl.ds(b0, nb), :, :]
    return jnp.broadcast_to(v, (nb, SUBLANES, v.shape[-1])).reshape(rows, v.shape[-1])


def _store_rows(ref, r0, rows, bb, val):
    if bb == 1:
        ref[0, pl.ds(r0, rows), :] = val.astype(ref.dtype)
    else:
        nb = rows // SUBLANES
        b0 = pl.multiple_of(r0 // SUBLANES, nb)
        ref[pl.ds(b0, nb), :, :] = val.reshape(nb, SUBLANES, val.shape[-1]).astype(ref.dtype)


def _tile_maps(T, tt):
    tpb = T // tt
    x_map = lambda m, *_: (m // tpb, m % tpb, 0)
    mod_map = lambda m, *_: (m // tpb, 0, 0)
    return x_map, mod_map


ROW_CHUNK = 256


def _ada_kernel(c_ref, w_ref, b_ref, o_ref):
    s = _silu(c_ref[...])
    o_ref[...] = _bdot(s, w_ref[...]) + b_ref[...]


def _ada(c, w, b, tn=1024):
    M, K = c.shape
    N = w.shape[1]
    return pl.pallas_call(
        _ada_kernel,
        grid=(N // tn,),
        in_specs=[pl.BlockSpec((M, K), lambda n: (0, 0)),
                  pl.BlockSpec((K, tn), lambda n: (0, n)),
                  pl.BlockSpec((1, tn), lambda n: (0, n))],
        out_specs=pl.BlockSpec((M, tn), lambda n: (0, n)),
        out_shape=jax.ShapeDtypeStruct((M, N), F32),
        compiler_params=_params(("arbitrary",)),
        name="ada",
    )(c, w, b.reshape(1, N))


def _inproj_kernel(x_ref, sh_ref, sc_ref, g_ref, w_ref, wg_ref, o_ref, og_ref, h_scr, *, bb, tm):
    @pl.when(pl.program_id(1) == 0)
    def _():
        def body(i, carry):
            r0 = pl.multiple_of(i * ROW_CHUNK, ROW_CHUNK)
            x = _rows(x_ref, r0, ROW_CHUNK, bb)
            h = _rms(x, g_ref[...]) * (1.0 + _mod_rows(sc_ref, r0, ROW_CHUNK, bb)) \
                + _mod_rows(sh_ref, r0, ROW_CHUNK, bb)
            h_scr[pl.ds(r0, ROW_CHUNK), :] = h.astype(BF16)
            return carry
        lax.fori_loop(0, tm // ROW_CHUNK, body, 0)
        og_ref[...] = jnp.dot(h_scr[...], wg_ref[...], preferred_element_type=F32)

    o_ref[...] = jnp.dot(h_scr[...], w_ref[...], preferred_element_type=F32).astype(o_ref.dtype)


def _inproj(x, sh, sc, g, w, wg, bb, tt, out_dtype, tn=1024):
    B, T, D = x.shape
    tm = bb * tt
    n_m = (B * T) // tm
    N = w.shape[1]
    x_map, mod_map = _tile_maps(T, tt)
    return pl.pallas_call(
        functools.partial(_inproj_kernel, bb=bb, tm=tm),
        grid=(n_m, N // tn),
        in_specs=[pl.BlockSpec((bb, tt, D), x_map),
                  pl.BlockSpec((bb, 1, D), mod_map),
                  pl.BlockSpec((bb, 1, D), mod_map),
                  pl.BlockSpec((1, D), lambda m, n: (0, 0)),
                  pl.BlockSpec((D, tn), lambda m, n: (0, n)),
                  pl.BlockSpec((D, LANES), lambda m, n: (0, 0))],
        out_specs=[pl.BlockSpec((tm, tn), lambda m, n: (m, n)),
                   pl.BlockSpec((tm, LANES), lambda m, n: (m, 0))],
        out_shape=[jax.ShapeDtypeStruct((B * T, N), out_dtype),
                   jax.ShapeDtypeStruct((B * T, LANES), F32)],
        scratch_shapes=[pltpu.VMEM((tm, D), BF16)],
        compiler_params=_params(("arbitrary", "arbitrary")),
        name="inproj",
    )(x, sh, sc, g, w, wg)


def _seq_row0(j, rows):
    return j * rows if isinstance(j, int) else pl.multiple_of(j * rows, rows)


def _tri(L):
    row = lax.broadcasted_iota(jnp.int32, (L, L), 0)
    col = lax.broadcasted_iota(jnp.int32, (L, L), 1)
    return col <= row, col < row, jnp.where(col <= row, 1.0, 0.0), jnp.where(row <= col, 1.0, 0.0)


def _head_norm_gate(hs, gs, gates):
    ms = [jnp.mean(h * h, axis=-1, keepdims=True) for h in hs]
    return [h * lax.rsqrt(v + EPS) * g * gate for h, v, g, gate in zip(hs, ms, gs, gates)]


def _for_each_group(n, fn):
    if n == 1:
        fn(0)
    else:
        def body(j, carry):
            fn(j)
            return carry
        lax.fori_loop(0, n, body, 0)


def _scan_specs(nsub, nc, L, n_steps, col0):
    rows = nsub * nc * L
    rb = lambda i, c: i * n_steps + c
    col_specs = [pl.BlockSpec((rows, HALF), functools.partial(lambda i, c, j: (rb(i, c), j), j=col0 + j))
                 for j in range(4)]
    gate_specs = [pl.BlockSpec((1, nsub, nc * L, N_GATE), lambda i, c: (rb(i, c), 0, 0, 0)),
                  pl.BlockSpec((1, nsub, nc, N_GATE, L), lambda i, c: (rb(i, c), 0, 0, 0, 0))]
    return rows, rb, col_specs, gate_specs


def _mlstm_kernel(q_ref, k_ref, v_ref, o_ref, gc_ref, gr_ref, bc_ref, br_ref, gn_ref,
                  C0_ref, n0_ref, m0_ref, h_ref, C_ref, n_ref, m_ref, *, L, nc, nsub, jb):
    @pl.when(pl.program_id(1) == 0)
    def _():
        C_ref[...] = C0_ref[...]
        n_ref[...] = n0_ref[...]
        m_ref[...] = m0_ref[...]

    tril, _, tril_f, triu_f = _tri(L)
    lane_m = lax.broadcasted_iota(jnp.int32, (1, N_HEADS), 1)
    H = range(N_HEADS)
    hs = [slice(h * D_HEAD, (h + 1) * D_HEAD) for h in H]

    TS = range(jb)
    U = [(t, h) for t in TS for h in H]

    def group(jg):
        js = [jg * jb + t for t in TS]
        m_rows = [m_ref[j] for j in js]
        for c in range(nc):
            rows = [pl.ds(_seq_row0(j, nc * L) + c * L, L) for j in js]
            zc = [gc_ref[0, j, c * L:(c + 1) * L, :] + bc_ref[...] for j in js]
            zr = [gr_ref[0, j, c] + br_ref[...] for j in js]
            lf_c = [_log_sigmoid(z[:, N_HEADS:N_GATE]) for z in zc]
            lf_r = [_log_sigmoid(z[N_HEADS:N_GATE, :]) for z in zr]
            b_c = [_prefix_dot(tril_f, x, None) for x in lf_c]
            b_r = [_prefix_dot(None, x, triu_f) for x in lf_r]
            a_c = [zc[t][:, 0:N_HEADS] - b_c[t] for t in TS]
            a_r = [zr[t][0:N_HEADS, :] - b_r[t] for t in TS]
            q = [q_ref[rows[t], hs[h]].astype(F32) for t, h in U]
            ks = [k_ref[rows[t], hs[h]].astype(F32) * (D_HEAD ** -0.5) for t, h in U]
            qb = [x.astype(BF16) for x in q]
            kb = [x.astype(BF16) for x in ks]
            vb = [v_ref[rows[t], hs[h]].astype(BF16) for t, h in U]
            N = range(len(U))
            qk = [lax.dot_general(qb[u], kb[u], NT, preferred_element_type=F32) for u in N]
            ar = [a_r[t][h:h + 1, :] for t, h in U]
            A_c = [jnp.max(jnp.where(tril, ar[u], -jnp.inf), axis=1, keepdims=True) for u in N]
            p0 = [qk[u] * jnp.exp(jnp.where(tril, ar[u] - A_c[u], -jnp.inf)) for u in N]
            rs = [jnp.sum(p0[u], axis=1, keepdims=True) for u in N]
            sv = [jnp.dot(p0[u].astype(BF16), vb[u], preferred_element_type=F32) for u in N]
            A_last = [A_c[u][L - 1:L, :] for u in N]
            kw = [ks[u] * jnp.exp(a_c[t][:, h:h + 1] - A_last[u]) for u, (t, h) in enumerate(U)]
            kv = [lax.dot_general(kw[u].astype(BF16), vb[u], TN, preferred_element_type=F32) for u in N]
            ksum = [jnp.sum(kw[u], axis=0, keepdims=True) for u in N]
            C = [C_ref[js[t], h] for t, h in U]
            n = [n_ref[js[t], h:h + 1, :] for t, h in U]
            m = [m_rows[t][:, h:h + 1] for t, h in U]
            bc = [b_c[t][:, h:h + 1] for t, h in U]
            qC = [jnp.dot(qb[u], C[u].astype(BF16), preferred_element_type=F32) for u in N]
            M_c = [jnp.maximum(m[u], A_c[u]) for u in N]
            r = [jnp.exp(A_c[u] - M_c[u]) for u in N]
            dec = [jnp.exp(m[u] - M_c[u]) for u in N]
            qn = [jnp.sum(q[u] * n[u], axis=1, keepdims=True) for u in N]
            numer = [dec[u] * qC[u] + r[u] * sv[u] for u in N]
            dd = [dec[u] * qn[u] + r[u] * rs[u] for u in N]
            floor = [jnp.exp(-(bc[u] + M_c[u])) for u in N]
            hh = [numer[u] / jnp.maximum(jnp.abs(dd[u]), floor[u]) for u in N]
            M_last = [M_c[u][L - 1:L, :] for u in N]
            rl = [jnp.exp(A_last[u] - M_last[u]) for u in N]
            dC = [jnp.exp(m[u] - M_last[u]) for u in N]
            gate = [jax.nn.sigmoid(o_ref[rows[t], hs[h]].astype(F32)) for t, h in U]
            out = _head_norm_gate(hh, [gn_ref[:, hs[h]] for t, h in U], gate)
            for u, (t, h) in enumerate(U):
                C_ref[js[t], h] = dC[u] * C[u] + rl[u] * kv[u]
                n_ref[js[t], h:h + 1, :] = dC[u] * n[u] + rl[u] * ksum[u]
                m_rows[t] = jnp.where(lane_m == h, bc[u][L - 1:L, :] + M_last[u], m_rows[t])
                h_ref[rows[t], hs[h]] = out[u].astype(h_ref.dtype)
        for t in TS:
            m_ref[js[t]] = m_rows[t]

    _for_each_group(nsub // jb, group)


def _mlstm(proj, gc, gr, bias, gn, C0, n0, m0, L, nc, nsub, jb, act_dtype):
    B = C0.shape[0]
    n_tok = proj.shape[0]
    n_bblk = B // nsub
    n_steps = n_tok // (n_bblk * nsub * nc * L)
    rows, rb, col_specs, gate_specs = _scan_specs(nsub, nc, L, n_steps, 0)
    const2 = lambda i, c: (0, 0)
    st4 = pl.BlockSpec((nsub, N_HEADS, D_HEAD, D_HEAD), lambda i, c: (i, 0, 0, 0))
    st3 = pl.BlockSpec((nsub, N_HEADS, D_HEAD), lambda i, c: (i, 0, 0))
    stm = pl.BlockSpec((nsub, 1, N_HEADS), lambda i, c: (i, 0, 0))
    return pl.pallas_call(
        functools.partial(_mlstm_kernel, L=L, nc=nc, nsub=nsub, jb=jb),
        grid=(n_bblk, n_steps),
        in_specs=col_specs + gate_specs + [
            pl.BlockSpec((1, N_GATE), const2), pl.BlockSpec((N_GATE, 1), const2),
            pl.BlockSpec((1, HALF), const2), st4, st3, stm],
        out_specs=[pl.BlockSpec((rows, HALF), lambda i, c: (rb(i, c), 0)), st4, st3, stm],
        out_shape=[jax.ShapeDtypeStruct((n_tok, HALF), act_dtype),
                   jax.ShapeDtypeStruct(C0.shape, F32),
                   jax.ShapeDtypeStruct(n0.shape, F32),
                   jax.ShapeDtypeStruct(m0.shape, F32)],
        compiler_params=_params(("arbitrary", "arbitrary")),
        name="mlstm",
    )(proj, proj, proj, proj, gc, gr, bias.reshape(1, N_GATE), bias.reshape(N_GATE, 1),
      gn.reshape(1, HALF), C0, n0, m0)


def _conv_silu(u, prev, w_ref, cs, first_rows):
    R = u.shape[0]
    out = u * w_ref[CONV_W - 1:CONV_W, cs]
    for j in range(1, CONV_W):
        shifted = pltpu.roll(u, j, 0)
        head = jnp.where(first_rows < j, pltpu.roll(prev, j, 0), shifted[0:SUBLANES, :])
        if R > SUBLANES:
            shifted = jnp.concatenate([head, shifted[SUBLANES:, :]], axis=0)
        else:
            shifted = head
        out = out + shifted * w_ref[CONV_W - 1 - j:CONV_W - j, cs]
    return _silu(out)


def _l2n(x):
    return x * lax.rsqrt(jnp.sum(x * x, axis=-1, keepdims=True) + EPS)


def _unit_lower_inverse(mats):
    L = mats[0].shape[0]
    B8 = SUBLANES
    n = range(len(mats))
    lane = lax.broadcasted_iota(jnp.int32, (B8, L), 1)
    sub = lax.broadcasted_iota(jnp.int32, (B8, L), 0)
    blk = lane // B8
    packed = []
    for a in mats:
        p = a[0:B8, :]
        for b in range(1, L // B8):
            p = jnp.where(blk == b, a[b * B8:(b + 1) * B8, :], p)
        packed.append(p)
    eye8 = jnp.where(lane - blk * B8 == sub, 1.0, 0.0)
    x = [eye8 for _ in n]
    for s in range(B8 - 1):
        cols = [jnp.take_along_axis(packed[i], blk * B8 + s, axis=1) for i in n]
        x = [x[i] - cols[i] * x[i][s:s + 1, :] for i in n]
    if L == B8:
        return x
    d = [jnp.concatenate([jnp.where(blk == b, x[i], 0.0) for b in range(L // B8)], axis=0) for i in n]
    row = lax.broadcasted_iota(jnp.int32, (L, L), 0)
    colm = lax.broadcasted_iota(jnp.int32, (L, L), 1)
    size = B8
    while size < L:
        rb, cb = row // size, colm // size
        join = (rb // 2 == cb // 2) & (rb != cb)
        ad = [_dot3(jnp.where(join, mats[i], 0.0), d[i]) for i in n]
        dad = [_dot3(d[i], ad[i]) for i in n]
        d = [d[i] - dad[i] for i in n]
        size *= 2
    return d


def _gdn_kernel(q_ref, k_ref, v_ref, z_ref, gc_ref, gr_ref, ac_ref, ar_ref, dc_ref, dr_ref,
                cw_ref, gn_ref, S0_ref, cv0_ref, h_ref, S_ref, cv_ref,
                qkv_scr, wq_scr, u_scr, qk_scr, kd_scr, gs_scr, *, L, nc, nsub, jb):
    @pl.when(pl.program_id(1) == 0)
    def _():
        S_ref[...] = S0_ref[...]
        cv_ref[...] = cv0_ref[...]

    R = nc * L
    tril, strict, tril_f, triu_f = _tri(L)
    first_rows = lax.broadcasted_iota(jnp.int32, (SUBLANES, HALF), 0)
    H = range(N_HEADS)
    hs = [slice(h * D_HEAD, (h + 1) * D_HEAD) for h in H]

    TS = range(jb)
    U = [(t, h) for t in TS for h in H]
    N = range(len(U))

    def group(jg):
        js = [jg * jb + t for t in TS]
        r0 = [_seq_row0(j, R) for j in js]
        for t in TS:
            for p, ref in enumerate((q_ref, k_ref, v_ref)):
                cs = slice(p * HALF, (p + 1) * HALF)
                u = ref[pl.ds(r0[t], R), :].astype(F32)
                prev = cv_ref[js[t], :, cs]
                qkv_scr[t * 3 + p] = _conv_silu(u, prev, cw_ref, cs, first_rows)
                cv_ref[js[t], :, cs] = u[R - SUBLANES:, :]
        for c in range(nc):
            rc = slice(c * L, (c + 1) * L)
            zc = [gc_ref[0, j, rc, :] for j in js]
            zr = [gr_ref[0, j, c] for j in js]
            g_c = [-jnp.exp(ac_ref[...]) * _softplus(z[:, 0:N_HEADS] + dc_ref[...]) for z in zc]
            g_r = [-jnp.exp(ar_ref[...]) * _softplus(z[0:N_HEADS, :] + dr_ref[...]) for z in zr]
            beta_t = [jax.nn.sigmoid(z[:, N_HEADS:N_GATE]) for z in zc]
            G_c = [_prefix_dot(tril_f, x, None) for x in g_c]
            G_r = [_prefix_dot(None, x, triu_f) for x in g_r]
            gam_t = [jnp.exp(x) for x in G_c]
            G_last = [x[L - 1:L, :] for x in G_c]
            kdec_t = [jnp.exp(G_last[t] - G_c[t]) for t in TS]
            for t in TS:
                gs_scr[t * nc + c] = jnp.exp(G_last[t])
            beta = [beta_t[t][:, h:h + 1] for t, h in U]
            gam = [gam_t[t][:, h:h + 1] for t, h in U]
            qn = [_l2n(qkv_scr[t * 3, rc, hs[h]]) * (D_HEAD ** -0.5) for t, h in U]
            kn = [_l2n(qkv_scr[t * 3 + 1, rc, hs[h]]) for t, h in U]
            kb = [x.astype(BF16) for x in kn]
            kq = [lax.dot_general(jnp.concatenate([kn[u], qn[u]], axis=0).astype(BF16), kb[u], NT,
                                  preferred_element_type=F32) for u in N]
            decay = [jnp.exp(jnp.where(tril, G_c[t][:, h:h + 1] - G_r[t][h:h + 1, :], -jnp.inf)) for t, h in U]
            a = [jnp.where(strict, beta[u] * decay[u] * kq[u][0:L, :], 0.0) for u in N]
            for u, (t, h) in enumerate(U):
                qk_scr[t * nc + c, h] = kq[u][L:2 * L, :] * decay[u]
                kd_scr[t * nc + c, h] = kn[u] * kdec_t[t][:, h:h + 1]
            T = _unit_lower_inverse(a)
            rhs = [jnp.concatenate([beta[u] * gam[u] * kn[u], beta[u] * qkv_scr[t * 3 + 2, rc, hs[h]]], axis=1)
                   for u, (t, h) in enumerate(U)]
            wu = [_dot3(T[u], rhs[u]) for u in N]
            for u, (t, h) in enumerate(U):
                wq_scr[t * nc + c, h] = jnp.concatenate([wu[u][:, 0:D_HEAD], gam[u] * qn[u]], axis=0).astype(BF16)
                u_scr[t * nc + c, h] = wu[u][:, D_HEAD:]
        for c in range(nc):
            rows = [pl.ds(r0[t] + c * L, L) for t in TS]
            S = [S_ref[js[t], h] for t, h in U]
            gS = [gs_scr[t * nc + c] for t in TS]
            P = [jnp.dot(wq_scr[t * nc + c, h], S[u].astype(BF16), preferred_element_type=F32)
                 for u, (t, h) in enumerate(U)]
            ub = [(u_scr[t * nc + c, h] - P[u][0:L, :]).astype(BF16) for u, (t, h) in enumerate(U)]
            o = [P[u][L:2 * L, :] + jnp.dot(qk_scr[t * nc + c, h].astype(BF16), ub[u], preferred_element_type=F32)
                 for u, (t, h) in enumerate(U)]
            dS = [lax.dot_general(kd_scr[t * nc + c, h].astype(BF16), ub[u], TN, preferred_element_type=F32)
                  for u, (t, h) in enumerate(U)]
            gate = [_silu(z_ref[rows[t], hs[h]].astype(F32)) for t, h in U]
            out = _head_norm_gate(o, [gn_ref[:, hs[h]] for t, h in U], gate)
            for u, (t, h) in enumerate(U):
                S_ref[js[t], h] = gS[t][:, h:h + 1] * S[u] + dS[u]
                h_ref[rows[t], hs[h]] = out[u].astype(h_ref.dtype)

    _for_each_group(nsub // jb, group)


def _gdn(proj, gc, gr, A_log, dt_bias, conv_w, gn, S0, cv0, L, nc, nsub, jb, act_dtype):
    B = S0.shape[0]
    n_tok = proj.shape[0]
    n_bblk = B // nsub
    n_steps = n_tok // (n_bblk * nsub * nc * L)
    rows, rb, col_specs, gate_specs = _scan_specs(nsub, nc, L, n_steps, 4)
    const2 = lambda i, c: (0, 0)
    st4 = pl.BlockSpec((nsub, N_HEADS, D_HEAD, D_HEAD), lambda i, c: (i, 0, 0, 0))
    stc = pl.BlockSpec((nsub, SUBLANES, C_CONV), lambda i, c: (i, 0, 0))
    return pl.pallas_call(
        functools.partial(_gdn_kernel, L=L, nc=nc, nsub=nsub, jb=jb),
        grid=(n_bblk, n_steps),
        in_specs=col_specs + gate_specs + [
            pl.BlockSpec((1, N_HEADS), const2), pl.BlockSpec((N_HEADS, 1), const2),
            pl.BlockSpec((1, N_HEADS), const2), pl.BlockSpec((N_HEADS, 1), const2),
            pl.BlockSpec((CONV_W, C_CONV), const2), pl.BlockSpec((1, HALF), const2), st4, stc],
        out_specs=[pl.BlockSpec((rows, HALF), lambda i, c: (rb(i, c), 0)), st4, stc],
        out_shape=[jax.ShapeDtypeStruct((n_tok, HALF), act_dtype),
                   jax.ShapeDtypeStruct(S0.shape, F32),
                   jax.ShapeDtypeStruct(cv0.shape, F32)],
        scratch_shapes=[pltpu.VMEM((jb * 3, nc * L, HALF), F32),
                        pltpu.VMEM((jb * nc, N_HEADS, 2 * L, D_HEAD), BF16),
                        pltpu.VMEM((jb * nc, N_HEADS, L, D_HEAD), F32),
                        pltpu.VMEM((jb * nc, N_HEADS, L, L), F32),
                        pltpu.VMEM((jb * nc, N_HEADS, L, D_HEAD), F32),
                        pltpu.VMEM((jb * nc, 1, N_HEADS), F32)],
        compiler_params=_params(("arbitrary", "arbitrary")),
        name="gdn",
    )(proj, proj, proj, proj, gc, gr,
      A_log.reshape(1, N_HEADS), A_log.reshape(N_HEADS, 1),
      dt_bias.reshape(1, N_HEADS), dt_bias.reshape(N_HEADS, 1),
      conv_w, gn.reshape(1, HALF), S0, cv0)


def _outproj_kernel(ha_ref, hb_ref, wa_ref, wb_ref, x_ref, g1_ref, sh_ref, sc_ref, gn_ref,
                    x1_ref, h2_ref, *, bb, tm):
    def body(i, carry):
        r0 = pl.multiple_of(i * ROW_CHUNK, ROW_CHUNK)
        mix = _bdot(ha_ref[pl.ds(r0, ROW_CHUNK), :], wa_ref[...]) \
            + _bdot(hb_ref[pl.ds(r0, ROW_CHUNK), :], wb_ref[...])
        x1 = _rows(x_ref, r0, ROW_CHUNK, bb) + _mod_rows(g1_ref, r0, ROW_CHUNK, bb) * mix
        _store_rows(x1_ref, r0, ROW_CHUNK, bb, x1)
        h2 = _rms(x1, gn_ref[...]) * (1.0 + _mod_rows(sc_ref, r0, ROW_CHUNK, bb)) \
            + _mod_rows(sh_ref, r0, ROW_CHUNK, bb)
        h2_ref[pl.ds(r0, ROW_CHUNK), :] = h2.astype(BF16)
        return carry
    lax.fori_loop(0, tm // ROW_CHUNK, body, 0)


def _outproj(ha, hb, w_out, x, g1, sh2, sc2, gn, bb, tt):
    B, T, D = x.shape
    tm = bb * tt
    x_map, mod_map = _tile_maps(T, tt)
    mod_spec = pl.BlockSpec((bb, 1, D), mod_map)
    return pl.pallas_call(
        functools.partial(_outproj_kernel, bb=bb, tm=tm),
        grid=((B * T) // tm,),
        in_specs=[pl.BlockSpec((tm, HALF), lambda m: (m, 0)),
                  pl.BlockSpec((tm, HALF), lambda m: (m, 0)),
                  pl.BlockSpec((HALF, D), lambda m: (0, 0)),
                  pl.BlockSpec((HALF, D), lambda m: (1, 0)),
                  pl.BlockSpec((bb, tt, D), x_map),
                  mod_spec, mod_spec, mod_spec,
                  pl.BlockSpec((1, D), lambda m: (0, 0))],
        out_specs=[pl.BlockSpec((bb, tt, D), x_map),
                   pl.BlockSpec((tm, D), lambda m: (m, 0))],
        out_shape=[jax.ShapeDtypeStruct((B, T, D), F32),
                   jax.ShapeDtypeStruct((B * T, D), BF16)],
        compiler_params=_params(("arbitrary",)),
        name="outproj",
    )(ha, hb, w_out, w_out, x, g1, sh2, sc2, gn)


def _mlp_kernel(h_ref, wu_ref, wd_ref, o_ref, wu_scr, wd_scr, *, tm):
    wu_scr[...] = wu_ref[...].astype(BF16)
    wd_scr[...] = wd_ref[...].astype(BF16)

    @pl.when(pl.program_id(1) == 0)
    def _():
        o_ref[...] = jnp.zeros_like(o_ref)

    def body(i, carry):
        r0 = pl.multiple_of(i * ROW_CHUNK, ROW_CHUNK)
        a = jnp.dot(h_ref[pl.ds(r0, ROW_CHUNK), :], wu_scr[...], preferred_element_type=F32)
        a = jnp.square(jnp.maximum(a, 0.0)).astype(BF16)
        o_ref[pl.ds(r0, ROW_CHUNK), :] += jnp.dot(a, wd_scr[...], preferred_element_type=F32)
        return carry
    lax.fori_loop(0, tm // ROW_CHUNK, body, 0, unroll=True)


def _mlp(h2, w_up, w_down, tm, tf=512):
    n_tok, D = h2.shape
    return pl.pallas_call(
        functools.partial(_mlp_kernel, tm=tm),
        grid=(n_tok // tm, D_FF // tf),
        in_specs=[pl.BlockSpec((tm, D), lambda m, f: (m, 0)),
                  pl.BlockSpec((D, tf), lambda m, f: (0, f)),
                  pl.BlockSpec((tf, D), lambda m, f: (f, 0))],
        out_specs=pl.BlockSpec((tm, D), lambda m, f: (m, 0)),
        out_shape=jax.ShapeDtypeStruct((n_tok, D), F32),
        scratch_shapes=[pltpu.VMEM((D, tf), BF16), pltpu.VMEM((tf, D), BF16)],
        compiler_params=_params(("arbitrary", "arbitrary")),
        name="mlp",
    )(h2, w_up, w_down)


def _final_kernel(x1_ref, y_ref, g2_ref, sh_ref, sc_ref, gn_ref, o_ref, *, bb, tm):
    def body(i, carry):
        r0 = pl.multiple_of(i * ROW_CHUNK, ROW_CHUNK)
        x2 = _rows(x1_ref, r0, ROW_CHUNK, bb) + _mod_rows(g2_ref, r0, ROW_CHUNK, bb) * y_ref[pl.ds(r0, ROW_CHUNK), :]
        out = _rms(x2, gn_ref[...]) * (1.0 + _mod_rows(sc_ref, r0, ROW_CHUNK, bb)) \
            + _mod_rows(sh_ref, r0, ROW_CHUNK, bb)
        _store_rows(o_ref, r0, ROW_CHUNK, bb, out)
        return carry
    lax.fori_loop(0, tm // ROW_CHUNK, body, 0)


def _final(x1, y, g2, shf, scf, gn, bb, tt):
    B, T, D = x1.shape
    tm = bb * tt
    x_map, mod_map = _tile_maps(T, tt)
    mod_spec = pl.BlockSpec((bb, 1, D), mod_map)
    return pl.pallas_call(
        functools.partial(_final_kernel, bb=bb, tm=tm),
        grid=((B * T) // tm,),
        in_specs=[pl.BlockSpec((bb, tt, D), x_map),
                  pl.BlockSpec((tm, D), lambda m: (m, 0)),
                  mod_spec, mod_spec, mod_spec,
                  pl.BlockSpec((1, D), lambda m: (0, 0))],
        out_specs=pl.BlockSpec((bb, tt, D), x_map),
        out_shape=jax.ShapeDtypeStruct((B, T, D), F32),
        compiler_params=_params(("arbitrary",)),
        name="final",
    )(x1, y, g2, shf, scf, gn)


def _gate_views(gates, off, n_blk, nsub, nc, L):
    g = gates[:, off:off + N_GATE].reshape(n_blk, nsub, nc, L, N_GATE)
    return g.reshape(n_blk, nsub, nc * L, N_GATE), jnp.swapaxes(g, 3, 4)


def _group_layer(x, mod, C0, n0, m0, S0, cv0, w, bb_big, tt_big, bb_small, tt_small, nc, nsub, jb, act_dtype):
    B, T, D = x.shape
    L = math.gcd(T, CHUNK)
    sh1, sc1, g1, sh2, sc2, g2 = [mod[:, i * D:(i + 1) * D].reshape(B, 1, D) for i in range(6)]
    proj, gates = _inproj(x, sh1, sc1, w["norm1"], w["w_in_main"], w["w_in_gate"], bb_big, tt_big, act_dtype)
    n_blk = (B * T) // (nsub * nc * L)
    gca, gra = _gate_views(gates, 0, n_blk, nsub, nc, L)
    gcb, grb = _gate_views(gates, N_GATE, n_blk, nsub, nc, L)
    ha, C1, n1, m1 = _mlstm(proj, gca, gra, w["gate_b"], w["mlstm_g"], C0, n0,
                            m0.reshape(B, 1, N_HEADS), L, nc, nsub, jb, act_dtype)
    cv0p = jnp.concatenate([jnp.zeros((B, SUBLANES - (CONV_W - 1), C_CONV), F32), cv0.astype(F32)], axis=1)
    hb, S1, cv1 = _gdn(proj, gcb, grb, w["A_log"], w["dt_bias"], w["conv_w"], w["gdn_g"], S0, cv0p,
                       L, nc, nsub, jb, act_dtype)
    x1, h2 = _outproj(ha, hb, w["w_out"], x, g1, sh2, sc2, w["norm2"], bb_small, tt_small)
    y = _mlp(h2, w["w_up"], w["w_down"], bb_big * tt_big)
    return x1, y, g2, C1, n1, m1.reshape(B, N_HEADS), S1, cv1[:, SUBLANES - (CONV_W - 1):, :]


def _regroup_w_in(w_in):
    o = 0
    seg = {}
    for name, width in (("qa", HALF), ("ka", HALF), ("va", HALF), ("ia", N_HEADS), ("fa", N_HEADS), ("oa", HALF),
                        ("qb", HALF), ("kb", HALF), ("vb", HALF), ("ab", N_HEADS), ("bb", N_HEADS), ("zb", HALF)):
        seg[name] = w_in[:, o:o + width]
        o += width
    main = jnp.concatenate([seg[n] for n in ("qa", "ka", "va", "oa", "qb", "kb", "vb", "zb")], axis=1)
    gate = jnp.concatenate([seg[n] for n in ("ia", "fa", "ab", "bb")]
                           + [jnp.zeros((w_in.shape[0], LANES - 2 * N_GATE), w_in.dtype)], axis=1)
    return main.astype(BF16), gate.astype(BF16)


def kernel(x_prompt, x_sample, state_mlstm_C, state_mlstm_n, state_mlstm_m, state_gdn_S, state_gdn_conv,
           c_prompt, c_sample, ada_w, ada_b, norm1, w_in, mlstm_gate_bias, mlstm_norm, gdn_conv_w,
           gdn_A_log, gdn_dt_bias, gdn_norm, w_out, norm2, w_up, w_down, ada_final_w, ada_final_b,
           norm_final):
    Bp, Tp, D = x_prompt.shape
    Bs, Ts, _ = x_sample.shape
    depth = ada_w.shape[0]
    n_c = Bp + Bs
    pad = (-n_c) % SUBLANES
    c_all = jnp.concatenate([c_prompt, c_sample, jnp.zeros((pad, D), F32)], axis=0)
    mod_f = _ada(c_all, ada_final_w, ada_final_b)

    xp, xs = x_prompt, x_sample
    outs_p, outs_s = [], []
    for l in range(depth):
        main, gate = _regroup_w_in(w_in[l])
        w = dict(norm1=norm1[l].reshape(1, D), w_in_main=main, w_in_gate=gate, gate_b=mlstm_gate_bias[l],
                 mlstm_g=mlstm_norm[l], conv_w=gdn_conv_w[l], A_log=gdn_A_log[l], dt_bias=gdn_dt_bias[l],
                 gdn_g=gdn_norm[l], w_out=w_out[l].astype(BF16), norm2=norm2[l].reshape(1, D),
                 w_up=w_up[l], w_down=w_down[l])
        mod = _ada(c_all, ada_w[l], ada_b[l])
        zC = jnp.zeros((Bp, N_HEADS, D_HEAD, D_HEAD), F32)
        rp = _group_layer(xp, mod[:Bp], zC, jnp.zeros((Bp, N_HEADS, D_HEAD), F32), jnp.zeros((Bp, N_HEADS), F32),
                          zC, jnp.zeros((Bp, CONV_W - 1, C_CONV), F32), w,
                          1, 1024, 1, 512, 2, 1, 1, BF16)
        rs = _group_layer(xs, mod[Bp:n_c], state_mlstm_C[l], state_mlstm_n[l], state_mlstm_m[l],
                          state_gdn_S[l], state_gdn_conv[l], w,
                          Bs, Ts, 64, Ts, 1, 8, 8, F32)
        last = l == depth - 1
        nf = norm_final.reshape(1, D)
        res = []
        for (x1, y, g2, *st), sl, bb, tt in ((rp, slice(0, Bp), 1, 512), (rs, slice(Bp, n_c), 64, Ts)):
            B = x1.shape[0]
            if last:
                shf = mod_f[sl, :D].reshape(B, 1, D)
                scf = mod_f[sl, D:].reshape(B, 1, D)
                xo = _final(x1, y, g2, shf, scf, nf, bb, tt)
            else:
                xo = x1 + g2 * y.reshape(x1.shape)
            res.append((xo, st))
        (xp, st_p), (xs, st_s) = res
        outs_p.append(st_p)
        outs_s.append(st_s)

    def stack(outs, i):
        return jnp.stack([o[i] for o in outs])

    return (xp, xs,
            stack(outs_p, 0), stack(outs_p, 1), stack(outs_p, 2), stack(outs_p, 3), stack(outs_p, 4),
            stack(outs_s, 0), stack(outs_s, 1), stack(outs_s, 2), stack(outs_s, 3), stack(outs_s, 4))
```

```python
import functools
import math

import jax
import jax.numpy as jnp
from jax import lax
from jax.experimental import pallas as pl
from jax.experimental.pallas import tpu as pltpu

F32 = jnp.float32
BF16 = jnp.bfloat16

D_MODEL = 2048
N_HEADS = 8
D_HEAD = 128
HALF = N_HEADS * D_HEAD
CONV_W = 4
C_CONV = 3 * HALF
D_FF = 4 * D_MODEL
CHUNK = 64
EPS = 1e-6
N_GATE = 2 * N_HEADS

SUBLANES = 8
LANES = 128
VMEM_LIMIT = 56 * 1024 * 1024

NT = (((1,), (1,)), ((), ()))
TN = (((0,), (0,)), ((), ()))


def _params(sem):
    return pltpu.CompilerParams(dimension_semantics=sem, vmem_limit_bytes=VMEM_LIMIT)


def _silu(x):
    return x * jax.nn.sigmoid(x)


def _softplus(x):
    return jnp.maximum(x, 0.0) + jnp.log1p(jnp.exp(-jnp.abs(x)))


def _log_sigmoid(x):
    return jnp.minimum(x, 0.0) - jnp.log1p(jnp.exp(-jnp.abs(x)))


def _rms(x, g):
    return x * lax.rsqrt(jnp.mean(x * x, axis=-1, keepdims=True) + EPS) * g


def _bdot(a, b):
    return jnp.dot(a.astype(BF16), b.astype(BF16), preferred_element_type=F32)


def _split2(x):
    hi = x.astype(BF16)
    return hi, (x - hi.astype(F32)).astype(BF16)


def _dot3s(a2, b2):
    (ah, al), (bh, bl) = a2, b2
    d = lambda x, y: jnp.dot(x, y, preferred_element_type=F32)
    return d(ah, bh) + (d(ah, bl) + d(al, bh))


def _dot3(a, b):
    return _dot3s(_split2(a), _split2(b))


def _prefix_dot(ones_l, x, ones_r):
    hi, lo = _split2(x)
    lo2 = (x - hi.astype(F32) - lo.astype(F32)).astype(BF16)
    if ones_r is None:
        c = ones_l.astype(BF16)
        d = lambda y: jnp.dot(c, y, preferred_element_type=F32)
    else:
        c = ones_r.astype(BF16)
        d = lambda y: jnp.dot(y, c, preferred_element_type=F32)
    return d(hi) + (d(lo) + d(lo2))


def _rows(ref, r0, rows, bb):
    if bb == 1:
        return ref[0, pl.ds(r0, rows), :]
    nb = rows // SUBLANES
    b0 = pl.multiple_of(r0 // SUBLANES, nb)
    return ref[pl.ds(b0, nb), :, :].reshape(rows, ref.shape[-1])


def _mod_rows(ref, r0, rows, bb):
    if bb == 1:
        return ref[0]
    nb = rows // SUBLANES
    b0 = pl.multiple_of(r0 // SUBLANES, nb)
    v = ref[pl.ds(b0, nb), :, :]
    return jnp.broadcast_to(v, (nb, SUBLANES, v.shape[-1])).reshape(rows, v.shape[-1])


def _store_rows(ref, r0, rows, bb, val):
    if bb == 1:
        ref[0, pl.ds(r0, rows), :] = val.astype(ref.dtype)
    else:
        nb = rows // SUBLANES
        b0 = pl.multiple_of(r0 // SUBLANES, nb)
        ref[pl.ds(b0, nb), :, :] = val.reshape(nb, SUBLANES, val.shape[-1]).astype(ref.dtype)


def _tile_maps(T, tt):
    tpb = T // tt
    x_map = lambda m, *_: (m // tpb, m % tpb, 0)
    mod_map = lambda m, *_: (m // tpb, 0, 0)
    return x_map, mod_map


ROW_CHUNK = 256


def _ada_kernel(c_ref, w_ref, b_ref, o_ref):
    s = _silu(c_ref[...])
    o_ref[...] = _bdot(s, w_ref[...]) + b_ref[...]


def _ada(c, w, b, tn=1024):
    M, K = c.shape
    N = w.shape[1]
    return pl.pallas_call(
        _ada_kernel,
        grid=(N // tn,),
        in_specs=[pl.BlockSpec((M, K), lambda n: (0, 0)),
                  pl.BlockSpec((K, tn), lambda n: (0, n)),
                  pl.BlockSpec((1, tn), lambda n: (0, n))],
        out_specs=pl.BlockSpec((M, tn), lambda n: (0, n)),
        out_shape=jax.ShapeDtypeStruct((M, N), F32),
        compiler_params=_params(("arbitrary",)),
        name="ada",
    )(c, w, b.reshape(1, N))


def _inproj_kernel(x_ref, sh_ref, sc_ref, g_ref, w_ref, wg_ref, o_ref, og_ref, h_scr, *, bb, tm):
    @pl.when(pl.program_id(1) == 0)
    def _():
        def body(i, carry):
            r0 = pl.multiple_of(i * ROW_CHUNK, ROW_CHUNK)
            x = _rows(x_ref, r0, ROW_CHUNK, bb)
            h = _rms(x, g_ref[...]) * (1.0 + _mod_rows(sc_ref, r0, ROW_CHUNK, bb)) \
                + _mod_rows(sh_ref, r0, ROW_CHUNK, bb)
            h_scr[pl.ds(r0, ROW_CHUNK), :] = h.astype(BF16)
            return carry
        lax.fori_loop(0, tm // ROW_CHUNK, body, 0)
        og_ref[...] = jnp.dot(h_scr[...], wg_ref[...], preferred_element_type=F32)

    o_ref[...] = jnp.dot(h_scr[...], w_ref[...], preferred_element_type=F32).astype(o_ref.dtype)


def _inproj(x, sh, sc, g, w, wg, bb, tt, out_dtype, tn=1024):
    B, T, D = x.shape
    tm = bb * tt
    n_m = (B * T) // tm
    N = w.shape[1]
    x_map, mod_map = _tile_maps(T, tt)
    return pl.pallas_call(
        functools.partial(_inproj_kernel, bb=bb, tm=tm),
        grid=(n_m, N // tn),
        in_specs=[pl.BlockSpec((bb, tt, D), x_map),
                  pl.BlockSpec((bb, 1, D), mod_map),
                  pl.BlockSpec((bb, 1, D), mod_map),
                  pl.BlockSpec((1, D), lambda m, n: (0, 0)),
                  pl.BlockSpec((D, tn), lambda m, n: (0, n)),
                  pl.BlockSpec((D, LANES), lambda m, n: (0, 0))],
        out_specs=[pl.BlockSpec((tm, tn), lambda m, n: (m, n)),
                   pl.BlockSpec((tm, LANES), lambda m, n: (m, 0))],
        out_shape=[jax.ShapeDtypeStruct((B * T, N), out_dtype),
                   jax.ShapeDtypeStruct((B * T, LANES), F32)],
        scratch_shapes=[pltpu.VMEM((tm, D), BF16)],
        compiler_params=_params(("arbitrary", "arbitrary")),
        name="inproj",
    )(x, sh, sc, g, w, wg)


def _seq_row0(j, rows):
    return j * rows if isinstance(j, int) else pl.multiple_of(j * rows, rows)


def _tri(L):
    row = lax.broadcasted_iota(jnp.int32, (L, L), 0)
    col = lax.broadcasted_iota(jnp.int32, (L, L), 1)
    return col <= row, col < row, jnp.where(col <= row, 1.0, 0.0), jnp.where(row <= col, 1.0, 0.0)


def _head_norm_gate(hs, gs, gates):
    ms = [jnp.mean(h * h, axis=-1, keepdims=True) for h in hs]
    return [h * lax.rsqrt(v + EPS) * g * gate for h, v, g, gate in zip(hs, ms, gs, gates)]


def _for_each_group(n, fn):
    if n == 1:
        fn(0)
    else:
        def body(j, carry):
            fn(j)
            return carry
        lax.fori_loop(0, n, body, 0)


def _scan_specs(nsub, nc, L, n_steps, col0):
    rows = nsub * nc * L
    rb = lambda i, c: i * n_steps + c
    col_specs = [pl.BlockSpec((rows, HALF), functools.partial(lambda i, c, j: (rb(i, c), j), j=col0 + j))
                 for j in range(4)]
    gate_specs = [pl.BlockSpec((1, nsub, nc * L, N_GATE), lambda i, c: (rb(i, c), 0, 0, 0)),
                  pl.BlockSpec((1, nsub, nc, N_GATE, L), lambda i, c: (rb(i, c), 0, 0, 0, 0))]
    return rows, rb, col_specs, gate_specs


def _mlstm_kernel(q_ref, k_ref, v_ref, o_ref, gc_ref, gr_ref, bc_ref, br_ref, gn_ref,
                  C0_ref, n0_ref, m0_ref, h_ref, C_ref, n_ref, m_ref, *, L, nc, nsub, jb):
    @pl.when(pl.program_id(1) == 0)
    def _():
        C_ref[...] = C0_ref[...]
        n_ref[...] = n0_ref[...]
        m_ref[...] = m0_ref[...]

    tril, _, tril_f, triu_f = _tri(L)
    lane_m = lax.broadcasted_iota(jnp.int32, (1, N_HEADS), 1)
    H = range(N_HEADS)
    hs = [slice(h * D_HEAD, (h + 1) * D_HEAD) for h in H]

    TS = range(jb)
    U = [(t, h) for t in TS for h in H]

    def group(jg):
        js = [jg * jb + t for t in TS]
        m_rows = [m_ref[j] for j in js]
        for c in range(nc):
            rows = [pl.ds(_seq_row0(j, nc * L) + c * L, L) for j in js]
            zc = [gc_ref[0, j, c * L:(c + 1) * L, :] + bc_ref[...] for j in js]
            zr = [gr_ref[0, j, c] + br_ref[...] for j in js]
            lf_c = [_log_sigmoid(z[:, N_HEADS:N_GATE]) for z in zc]
            lf_r = [_log_sigmoid(z[N_HEADS:N_GATE, :]) for z in zr]
            b_c = [_prefix_dot(tril_f, x, None) for x in lf_c]
            b_r = [_prefix_dot(None, x, triu_f) for x in lf_r]
            a_c = [zc[t][:, 0:N_HEADS] - b_c[t] for t in TS]
            a_r = [zr[t][0:N_HEADS, :] - b_r[t] for t in TS]
            q = [q_ref[rows[t], hs[h]].astype(F32) for t, h in U]
            ks = [k_ref[rows[t], hs[h]].astype(F32) * (D_HEAD ** -0.5) for t, h in U]
            qb = [x.astype(BF16) for x in q]
            kb = [x.astype(BF16) for x in ks]
            vb = [v_ref[rows[t], hs[h]].astype(BF16) for t, h in U]
            N = range(len(U))
            qk = [lax.dot_general(qb[u], kb[u], NT, preferred_element_type=F32) for u in N]
            ar = [a_r[t][h:h + 1, :] for t, h in U]
            A_c = [jnp.max(jnp.where(tril, ar[u], -jnp.inf), axis=1, keepdims=True) for u in N]
            p0 = [qk[u] * jnp.exp(jnp.where(tril, ar[u] - A_c[u], -jnp.inf)) for u in N]
            rs = [jnp.sum(p0[u], axis=1, keepdims=True) for u in N]
            sv = [jnp.dot(p0[u].astype(BF16), vb[u], preferred_element_type=F32) for u in N]
            A_last = [A_c[u][L - 1:L, :] for u in N]
            kw = [ks[u] * jnp.exp(a_c[t][:, h:h + 1] - A_last[u]) for u, (t, h) in enumerate(U)]
            kv = [lax.dot_general(kw[u].astype(BF16), vb[u], TN, preferred_element_type=F32) for u in N]
            ksum = [jnp.sum(kw[u], axis=0, keepdims=True) for u in N]
            C = [C_ref[js[t], h] for t, h in U]
            n = [n_ref[js[t], h:h + 1, :] for t, h in U]
            m = [m_rows[t][:, h:h + 1] for t, h in U]
            bc = [b_c[t][:, h:h + 1] for t, h in U]
            qC = [jnp.dot(qb[u], C[u].astype(BF16), preferred_element_type=F32) for u in N]
            M_c = [jnp.maximum(m[u], A_c[u]) for u in N]
            r = [jnp.exp(A_c[u] - M_c[u]) for u in N]
            dec = [jnp.exp(m[u] - M_c[u]) for u in N]
            qn = [jnp.sum(q[u] * n[u], axis=1, keepdims=True) for u in N]
            numer = [dec[u] * qC[u] + r[u] * sv[u] for u in N]
            dd = [dec[u] * qn[u] + r[u] * rs[u] for u in N]
            floor = [jnp.exp(-(bc[u] + M_c[u])) for u in N]
            hh = [numer[u] / jnp.maximum(jnp.abs(dd[u]), floor[u]) for u in N]
            M_last = [M_c[u][L - 1:L, :] for u in N]
            rl = [jnp.exp(A_last[u] - M_last[u]) for u in N]
            dC = [jnp.exp(m[u] - M_last[u]) for u in N]
            gate = [jax.nn.sigmoid(o_ref[rows[t], hs[h]].astype(F32)) for t, h in U]
            out = _head_norm_gate(hh, [gn_ref[:, hs[h]] for t, h in U], gate)
            for u, (t, h) in enumerate(U):
                C_ref[js[t], h] = dC[u] * C[u] + rl[u] * kv[u]
                n_ref[js[t], h:h + 1, :] = dC[u] * n[u] + rl[u] * ksum[u]
                m_rows[t] = jnp.where(lane_m == h, bc[u][L - 1:L, :] + M_last[u], m_rows[t])
                h_ref[rows[t], hs[h]] = out[u].astype(h_ref.dtype)
        for t in TS:
            m_ref[js[t]] = m_rows[t]

    _for_each_group(nsub // jb, group)


def _mlstm(proj, gc, gr, bias, gn, C0, n0, m0, L, nc, nsub, jb, act_dtype):
    B = C0.shape[0]
    n_tok = proj.shape[0]
    n_bblk = B // nsub
    n_steps = n_tok // (n_bblk * nsub * nc * L)
    rows, rb, col_specs, gate_specs = _scan_specs(nsub, nc, L, n_steps, 0)
    const2 = lambda i, c: (0, 0)
    st4 = pl.BlockSpec((nsub, N_HEADS, D_HEAD, D_HEAD), lambda i, c: (i, 0, 0, 0))
    st3 = pl.BlockSpec((nsub, N_HEADS, D_HEAD), lambda i, c: (i, 0, 0))
    stm = pl.BlockSpec((nsub, 1, N_HEADS), lambda i, c: (i, 0, 0))
    return pl.pallas_call(
        functools.partial(_mlstm_kernel, L=L, nc=nc, nsub=nsub, jb=jb),
        grid=(n_bblk, n_steps),
        in_specs=col_specs + gate_specs + [
            pl.BlockSpec((1, N_GATE), const2), pl.BlockSpec((N_GATE, 1), const2),
            pl.BlockSpec((1, HALF), const2), st4, st3, stm],
        out_specs=[pl.BlockSpec((rows, HALF), lambda i, c: (rb(i, c), 0)), st4, st3, stm],
        out_shape=[jax.ShapeDtypeStruct((n_tok, HALF), act_dtype),
                   jax.ShapeDtypeStruct(C0.shape, F32),
                   jax.ShapeDtypeStruct(n0.shape, F32),
                   jax.ShapeDtypeStruct(m0.shape, F32)],
        compiler_params=_params(("arbitrary", "arbitrary")),
        name="mlstm",
    )(proj, proj, proj, proj, gc, gr, bias.reshape(1, N_GATE), bias.reshape(N_GATE, 1),
      gn.reshape(1, HALF), C0, n0, m0)


def _conv_silu(u, prev, w_ref, cs, first_rows):
    R = u.shape[0]
    out = u * w_ref[CONV_W - 1:CONV_W, cs]
    for j in range(1, CONV_W):
        shifted = pltpu.roll(u, j, 0)
        head = jnp.where(first_rows < j, pltpu.roll(prev, j, 0), shifted[0:SUBLANES, :])
        if R > SUBLANES:
            shifted = jnp.concatenate([head, shifted[SUBLANES:, :]], axis=0)
        else:
            shifted = head
        out = out + shifted * w_ref[CONV_W - 1 - j:CONV_W - j, cs]
    return _silu(out)


def _l2n(x):
    return x * lax.rsqrt(jnp.sum(x * x, axis=-1, keepdims=True) + EPS)


def _unit_lower_inverse(mats):
    L = mats[0].shape[0]
    B8 = SUBLANES
    n = range(len(mats))
    lane = lax.broadcasted_iota(jnp.int32, (B8, L), 1)
    sub = lax.broadcasted_iota(jnp.int32, (B8, L), 0)
    blk = lane // B8
    packed = []
    for a in mats:
        p = a[0:B8, :]
        for b in range(1, L // B8):
            p = jnp.where(blk == b, a[b * B8:(b + 1) * B8, :], p)
        packed.append(p)
    eye8 = jnp.where(lane - blk * B8 == sub, 1.0, 0.0)
    x = [eye8 for _ in n]
    for s in range(B8 - 1):
        cols = [jnp.take_along_axis(packed[i], blk * B8 + s, axis=1) for i in n]
        x = [x[i] - cols[i] * x[i][s:s + 1, :] for i in n]
    if L == B8:
        return x
    d = [jnp.concatenate([jnp.where(blk == b, x[i], 0.0) for b in range(L // B8)], axis=0) for i in n]
    row = lax.broadcasted_iota(jnp.int32, (L, L), 0)
    colm = lax.broadcasted_iota(jnp.int32, (L, L), 1)
    a2 = [_split2(a) for a in mats]
    size = B8
    while size < L:
        rb, cb = row // size, colm // size
        join = jnp.where((rb // 2 == cb // 2) & (rb != cb), 1.0, 0.0).astype(BF16)
        d2 = [_split2(x) for x in d]
        ad = [_dot3s((a2[i][0] * join, a2[i][1] * join), d2[i]) for i in n]
        dad = [_dot3s(d2[i], _split2(ad[i])) for i in n]
        d = [d[i] - dad[i] for i in n]
        size *= 2
    return d


def _gdn_kernel(q_ref, k_ref, v_ref, z_ref, gc_ref, gr_ref, ac_ref, ar_ref, dc_ref, dr_ref,
                cw_ref, gn_ref, S0_ref, cv0_ref, h_ref, S_ref, cv_ref,
                qkv_scr, wq_scr, u_scr, qk_scr, kd_scr, gs_scr, *, L, nc, nsub, jb):
    @pl.when(pl.program_id(1) == 0)
    def _():
        S_ref[...] = S0_ref[...]
        cv_ref[...] = cv0_ref[...]

    R = nc * L
    tril, strict, tril_f, triu_f = _tri(L)
    first_rows = lax.broadcasted_iota(jnp.int32, (SUBLANES, HALF), 0)
    H = range(N_HEADS)
    hs = [slice(h * D_HEAD, (h + 1) * D_HEAD) for h in H]

    TS = range(jb)
    U = [(t, h) for t in TS for h in H]
    N = range(len(U))

    def group(jg):
        js = [jg * jb + t for t in TS]
        r0 = [_seq_row0(j, R) for j in js]
        for t in TS:
            for p, ref in enumerate((q_ref, k_ref, v_ref)):
                cs = slice(p * HALF, (p + 1) * HALF)
                u = ref[pl.ds(r0[t], R), :].astype(F32)
                prev = cv_ref[js[t], :, cs]
                qkv_scr[t * 3 + p] = _conv_silu(u, prev, cw_ref, cs, first_rows)
                cv_ref[js[t], :, cs] = u[R - SUBLANES:, :]
        for c in range(nc):
            rc = slice(c * L, (c + 1) * L)
            zc = [gc_ref[0, j, rc, :] for j in js]
            zr = [gr_ref[0, j, c] for j in js]
            g_c = [-jnp.exp(ac_ref[...]) * _softplus(z[:, 0:N_HEADS] + dc_ref[...]) for z in zc]
            g_r = [-jnp.exp(ar_ref[...]) * _softplus(z[0:N_HEADS, :] + dr_ref[...]) for z in zr]
            beta_t = [jax.nn.sigmoid(z[:, N_HEADS:N_GATE]) for z in zc]
            G_c = [_prefix_dot(tril_f, x, None) for x in g_c]
            G_r = [_prefix_dot(None, x, triu_f) for x in g_r]
            gam_t = [jnp.exp(x) for x in G_c]
            G_last = [x[L - 1:L, :] for x in G_c]
            kdec_t = [jnp.exp(G_last[t] - G_c[t]) for t in TS]
            for t in TS:
                gs_scr[t * nc + c] = jnp.exp(G_last[t])
            beta = [beta_t[t][:, h:h + 1] for t, h in U]
            gam = [gam_t[t][:, h:h + 1] for t, h in U]
            qn = [_l2n(qkv_scr[t * 3, rc, hs[h]]) * (D_HEAD ** -0.5) for t, h in U]
            kn = [_l2n(qkv_scr[t * 3 + 1, rc, hs[h]]) for t, h in U]
            kb = [x.astype(BF16) for x in kn]
            kq = [lax.dot_general(jnp.concatenate([kn[u], qn[u]], axis=0).astype(BF16), kb[u], NT,
                                  preferred_element_type=F32) for u in N]
            decay = [jnp.exp(jnp.where(tril, G_c[t][:, h:h + 1] - G_r[t][h:h + 1, :], -jnp.inf)) for t, h in U]
            a = [jnp.where(strict, beta[u] * decay[u] * kq[u][0:L, :], 0.0) for u in N]
            for u, (t, h) in enumerate(U):
                qk_scr[t * nc + c, h] = kq[u][L:2 * L, :] * decay[u]
                kd_scr[t * nc + c, h] = kn[u] * kdec_t[t][:, h:h + 1]
            T = _unit_lower_inverse(a)
            rhs = [jnp.concatenate([beta[u] * gam[u] * kn[u], beta[u] * qkv_scr[t * 3 + 2, rc, hs[h]]], axis=1)
                   for u, (t, h) in enumerate(U)]
            wu = [_dot3(T[u], rhs[u]) for u in N]
            for u, (t, h) in enumerate(U):
                wq_scr[t * nc + c, h] = jnp.concatenate([wu[u][:, 0:D_HEAD], gam[u] * qn[u]], axis=0).astype(BF16)
                u_scr[t * nc + c, h] = wu[u][:, D_HEAD:]
        for c in range(nc):
            rows = [pl.ds(r0[t] + c * L, L) for t in TS]
            S = [S_ref[js[t], h] for t, h in U]
            gS = [gs_scr[t * nc + c] for t in TS]
            P = [jnp.dot(wq_scr[t * nc + c, h], S[u].astype(BF16), preferred_element_type=F32)
                 for u, (t, h) in enumerate(U)]
            ub = [(u_scr[t * nc + c, h] - P[u][0:L, :]).astype(BF16) for u, (t, h) in enumerate(U)]
            o = [P[u][L:2 * L, :] + jnp.dot(qk_scr[t * nc + c, h].astype(BF16), ub[u], preferred_element_type=F32)
                 for u, (t, h) in enumerate(U)]
            dS = [lax.dot_general(kd_scr[t * nc + c, h].astype(BF16), ub[u], TN, preferred_element_type=F32)
                  for u, (t, h) in enumerate(U)]
            gate = [_silu(z_ref[rows[t], hs[h]].astype(F32)) for t, h in U]
            out = _head_norm_gate(o, [gn_ref[:, hs[h]] for t, h in U], gate)
            for u, (t, h) in enumerate(U):
                S_ref[js[t], h] = gS[t][:, h:h + 1] * S[u] + dS[u]
                h_ref[rows[t], hs[h]] = out[u].astype(h_ref.dtype)

    _for_each_group(nsub // jb, group)


def _gdn(proj, gc, gr, A_log, dt_bias, conv_w, gn, S0, cv0, L, nc, nsub, jb, act_dtype):
    B = S0.shape[0]
    n_tok = proj.shape[0]
    n_bblk = B // nsub
    n_steps = n_tok // (n_bblk * nsub * nc * L)
    rows, rb, col_specs, gate_specs = _scan_specs(nsub, nc, L, n_steps, 4)
    const2 = lambda i, c: (0, 0)
    st4 = pl.BlockSpec((nsub, N_HEADS, D_HEAD, D_HEAD), lambda i, c: (i, 0, 0, 0))
    stc = pl.BlockSpec((nsub, SUBLANES, C_CONV), lambda i, c: (i, 0, 0))
    return pl.pallas_call(
        functools.partial(_gdn_kernel, L=L, nc=nc, nsub=nsub, jb=jb),
        grid=(n_bblk, n_steps),
        in_specs=col_specs + gate_specs + [
            pl.BlockSpec((1, N_HEADS), const2), pl.BlockSpec((N_HEADS, 1), const2),
            pl.BlockSpec((1, N_HEADS), const2), pl.BlockSpec((N_HEADS, 1), const2),
            pl.BlockSpec((CONV_W, C_CONV), const2), pl.BlockSpec((1, HALF), const2), st4, stc],
        out_specs=[pl.BlockSpec((rows, HALF), lambda i, c: (rb(i, c), 0)), st4, stc],
        out_shape=[jax.ShapeDtypeStruct((n_tok, HALF), act_dtype),
                   jax.ShapeDtypeStruct(S0.shape, F32),
                   jax.ShapeDtypeStruct(cv0.shape, F32)],
        scratch_shapes=[pltpu.VMEM((jb * 3, nc * L, HALF), F32),
                        pltpu.VMEM((jb * nc, N_HEADS, 2 * L, D_HEAD), BF16),
                        pltpu.VMEM((jb * nc, N_HEADS, L, D_HEAD), F32),
                        pltpu.VMEM((jb * nc, N_HEADS, L, L), F32),
                        pltpu.VMEM((jb * nc, N_HEADS, L, D_HEAD), F32),
                        pltpu.VMEM((jb * nc, 1, N_HEADS), F32)],
        compiler_params=_params(("arbitrary", "arbitrary")),
        name="gdn",
    )(proj, proj, proj, proj, gc, gr,
      A_log.reshape(1, N_HEADS), A_log.reshape(N_HEADS, 1),
      dt_bias.reshape(1, N_HEADS), dt_bias.reshape(N_HEADS, 1),
      conv_w, gn.reshape(1, HALF), S0, cv0)


def _outproj_kernel(ha_ref, hb_ref, wa_ref, wb_ref, x_ref, g1_ref, sh_ref, sc_ref, gn_ref,
                    x1_ref, h2_ref, *, bb, tm):
    def body(i, carry):
        r0 = pl.multiple_of(i * ROW_CHUNK, ROW_CHUNK)
        mix = _bdot(ha_ref[pl.ds(r0, ROW_CHUNK), :], wa_ref[...]) \
            + _bdot(hb_ref[pl.ds(r0, ROW_CHUNK), :], wb_ref[...])
        x1 = _rows(x_ref, r0, ROW_CHUNK, bb) + _mod_rows(g1_ref, r0, ROW_CHUNK, bb) * mix
        _store_rows(x1_ref, r0, ROW_CHUNK, bb, x1)
        h2 = _rms(x1, gn_ref[...]) * (1.0 + _mod_rows(sc_ref, r0, ROW_CHUNK, bb)) \
            + _mod_rows(sh_ref, r0, ROW_CHUNK, bb)
        h2_ref[pl.ds(r0, ROW_CHUNK), :] = h2.astype(BF16)
        return carry
    lax.fori_loop(0, tm // ROW_CHUNK, body, 0)


def _outproj(ha, hb, w_out, x, g1, sh2, sc2, gn, bb, tt):
    B, T, D = x.shape
    tm = bb * tt
    x_map, mod_map = _tile_maps(T, tt)
    mod_spec = pl.BlockSpec((bb, 1, D), mod_map)
    return pl.pallas_call(
        functools.partial(_outproj_kernel, bb=bb, tm=tm),
        grid=((B * T) // tm,),
        in_specs=[pl.BlockSpec((tm, HALF), lambda m: (m, 0)),
                  pl.BlockSpec((tm, HALF), lambda m: (m, 0)),
                  pl.BlockSpec((HALF, D), lambda m: (0, 0)),
                  pl.BlockSpec((HALF, D), lambda m: (1, 0)),
                  pl.BlockSpec((bb, tt, D), x_map),
                  mod_spec, mod_spec, mod_spec,
                  pl.BlockSpec((1, D), lambda m: (0, 0))],
        out_specs=[pl.BlockSpec((bb, tt, D), x_map),
                   pl.BlockSpec((tm, D), lambda m: (m, 0))],
        out_shape=[jax.ShapeDtypeStruct((B, T, D), F32),
                   jax.ShapeDtypeStruct((B * T, D), BF16)],
        compiler_params=_params(("arbitrary",)),
        name="outproj",
    )(ha, hb, w_out, w_out, x, g1, sh2, sc2, gn)


def _mlp_kernel(h_ref, wu_ref, wd_ref, o_ref, *wb_refs, tm):
    if wb_refs:
        wu_scr, wd_scr = wb_refs
        wu_scr[...] = wu_ref[...].astype(BF16)
        wd_scr[...] = wd_ref[...].astype(BF16)
    else:
        wu_scr, wd_scr = wu_ref, wd_ref

    @pl.when(pl.program_id(1) == 0)
    def _():
        o_ref[...] = jnp.zeros_like(o_ref)

    def body(i, carry):
        r0 = pl.multiple_of(i * ROW_CHUNK, ROW_CHUNK)
        a = jnp.dot(h_ref[pl.ds(r0, ROW_CHUNK), :], wu_scr[...], preferred_element_type=F32)
        a = jnp.square(jnp.maximum(a, 0.0)).astype(BF16)
        o_ref[pl.ds(r0, ROW_CHUNK), :] += jnp.dot(a, wd_scr[...], preferred_element_type=F32)
        return carry
    lax.fori_loop(0, tm // ROW_CHUNK, body, 0, unroll=True)


def _mlp(h2, w_up, w_down, tm, tf):
    n_tok, D = h2.shape
    emit_bf16 = w_up.dtype != BF16
    assert not emit_bf16 or n_tok == tm
    wu_spec = pl.BlockSpec((D, tf), lambda m, f: (0, f))
    wd_spec = pl.BlockSpec((tf, D), lambda m, f: (f, 0))
    y_spec = pl.BlockSpec((tm, D), lambda m, f: (m, 0))
    y_shape = jax.ShapeDtypeStruct((n_tok, D), F32)
    return pl.pallas_call(
        functools.partial(_mlp_kernel, tm=tm),
        grid=(n_tok // tm, D_FF // tf),
        in_specs=[pl.BlockSpec((tm, D), lambda m, f: (m, 0)), wu_spec, wd_spec],
        out_specs=[y_spec, wu_spec, wd_spec] if emit_bf16 else y_spec,
        out_shape=[y_shape, jax.ShapeDtypeStruct(w_up.shape, BF16), jax.ShapeDtypeStruct(w_down.shape, BF16)]
        if emit_bf16 else y_shape,
        compiler_params=_params(("arbitrary", "arbitrary")),
        name="mlp",
    )(h2, w_up, w_down)


def _final_kernel(x1_ref, y_ref, g2_ref, sh_ref, sc_ref, gn_ref, o_ref, *, bb, tm):
    def body(i, carry):
        r0 = pl.multiple_of(i * ROW_CHUNK, ROW_CHUNK)
        x2 = _rows(x1_ref, r0, ROW_CHUNK, bb) + _mod_rows(g2_ref, r0, ROW_CHUNK, bb) * y_ref[pl.ds(r0, ROW_CHUNK), :]
        out = _rms(x2, gn_ref[...]) * (1.0 + _mod_rows(sc_ref, r0, ROW_CHUNK, bb)) \
            + _mod_rows(sh_ref, r0, ROW_CHUNK, bb)
        _store_rows(o_ref, r0, ROW_CHUNK, bb, out)
        return carry
    lax.fori_loop(0, tm // ROW_CHUNK, body, 0)


def _final(x1, y, g2, shf, scf, gn, bb, tt):
    B, T, D = x1.shape
    tm = bb * tt
    x_map, mod_map = _tile_maps(T, tt)
    mod_spec = pl.BlockSpec((bb, 1, D), mod_map)
    return pl.pallas_call(
        functools.partial(_final_kernel, bb=bb, tm=tm),
        grid=((B * T) // tm,),
        in_specs=[pl.BlockSpec((bb, tt, D), x_map),
                  pl.BlockSpec((tm, D), lambda m: (m, 0)),
                  mod_spec, mod_spec, mod_spec,
                  pl.BlockSpec((1, D), lambda m: (0, 0))],
        out_specs=pl.BlockSpec((bb, tt, D), x_map),
        out_shape=jax.ShapeDtypeStruct((B, T, D), F32),
        compiler_params=_params(("arbitrary",)),
        name="final",
    )(x1, y, g2, shf, scf, gn)


def _gate_views(gates, off, n_blk, nsub, nc, L):
    g = gates[:, off:off + N_GATE].reshape(n_blk, nsub, nc, L, N_GATE)
    return g.reshape(n_blk, nsub, nc * L, N_GATE), jnp.swapaxes(g, 3, 4)


def _group_layer(x, mod, C0, n0, m0, S0, cv0, w, bb_big, tt_big, bb_small, tt_small, nc, nsub, jb, act_dtype):
    B, T, D = x.shape
    L = math.gcd(T, CHUNK)
    sh1, sc1, g1, sh2, sc2, g2 = [mod[:, i * D:(i + 1) * D].reshape(B, 1, D) for i in range(6)]
    proj, gates = _inproj(x, sh1, sc1, w["norm1"], w["w_in_main"], w["w_in_gate"], bb_big, tt_big, act_dtype)
    n_blk = (B * T) // (nsub * nc * L)
    gca, gra = _gate_views(gates, 0, n_blk, nsub, nc, L)
    gcb, grb = _gate_views(gates, N_GATE, n_blk, nsub, nc, L)
    ha, C1, n1, m1 = _mlstm(proj, gca, gra, w["gate_b"], w["mlstm_g"], C0, n0,
                            m0.reshape(B, 1, N_HEADS), L, nc, nsub, jb, act_dtype)
    cv0p = jnp.concatenate([jnp.zeros((B, SUBLANES - (CONV_W - 1), C_CONV), F32), cv0.astype(F32)], axis=1)
    hb, S1, cv1 = _gdn(proj, gcb, grb, w["A_log"], w["dt_bias"], w["conv_w"], w["gdn_g"], S0, cv0p,
                       L, nc, nsub, jb, act_dtype)
    x1, h2 = _outproj(ha, hb, w["w_out"], x, g1, sh2, sc2, w["norm2"], bb_small, tt_small)
    return x1, h2, g2, C1, n1, m1.reshape(B, N_HEADS), S1, cv1[:, SUBLANES - (CONV_W - 1):, :]


def _regroup_w_in(w_in):
    o = 0
    seg = {}
    for name, width in (("qa", HALF), ("ka", HALF), ("va", HALF), ("ia", N_HEADS), ("fa", N_HEADS), ("oa", HALF),
                        ("qb", HALF), ("kb", HALF), ("vb", HALF), ("ab", N_HEADS), ("bb", N_HEADS), ("zb", HALF)):
        seg[name] = w_in[:, o:o + width]
        o += width
    main = jnp.concatenate([seg[n] for n in ("qa", "ka", "va", "oa", "qb", "kb", "vb", "zb")], axis=1)
    gate = jnp.concatenate([seg[n] for n in ("ia", "fa", "ab", "bb")]
                           + [jnp.zeros((w_in.shape[0], LANES - 2 * N_GATE), w_in.dtype)], axis=1)
    return main.astype(BF16), gate.astype(BF16)


def kernel(x_prompt, x_sample, state_mlstm_C, state_mlstm_n, state_mlstm_m, state_gdn_S, state_gdn_conv,
           c_prompt, c_sample, ada_w, ada_b, norm1, w_in, mlstm_gate_bias, mlstm_norm, gdn_conv_w,
           gdn_A_log, gdn_dt_bias, gdn_norm, w_out, norm2, w_up, w_down, ada_final_w, ada_final_b,
           norm_final):
    Bp, Tp, D = x_prompt.shape
    Bs, Ts, _ = x_sample.shape
    depth = ada_w.shape[0]
    n_c = Bp + Bs
    pad = (-n_c) % SUBLANES
    c_all = jnp.concatenate([c_prompt, c_sample, jnp.zeros((pad, D), F32)], axis=0)
    mod_f = _ada(c_all, ada_final_w, ada_final_b)

    xp, xs = x_prompt, x_sample
    outs_p, outs_s = [], []
    for l in range(depth):
        main, gate = _regroup_w_in(w_in[l])
        w = dict(norm1=norm1[l].reshape(1, D), w_in_main=main, w_in_gate=gate, gate_b=mlstm_gate_bias[l],
                 mlstm_g=mlstm_norm[l], conv_w=gdn_conv_w[l], A_log=gdn_A_log[l], dt_bias=gdn_dt_bias[l],
                 gdn_g=gdn_norm[l], w_out=w_out[l].astype(BF16), norm2=norm2[l].reshape(1, D))
        mod = _ada(c_all, ada_w[l], ada_b[l])
        zC = jnp.zeros((Bp, N_HEADS, D_HEAD, D_HEAD), F32)
        rp = _group_layer(xp, mod[:Bp], zC, jnp.zeros((Bp, N_HEADS, D_HEAD), F32), jnp.zeros((Bp, N_HEADS), F32),
                          zC, jnp.zeros((Bp, CONV_W - 1, C_CONV), F32), w,
                          1, 1024, 1, 512, 2, 1, 1, BF16)
        rs = _group_layer(xs, mod[Bp:n_c], state_mlstm_C[l], state_mlstm_n[l], state_mlstm_m[l],
                          state_gdn_S[l], state_gdn_conv[l], w,
                          Bs, Ts, 64, Ts, 1, 8, 8, F32)
        ys, wu_b, wd_b = _mlp(rs[1], w_up[l], w_down[l], Bs * Ts, 512)
        yp = _mlp(rp[1], wu_b, wd_b, 1024, 1024)
        last = l == depth - 1
        nf = norm_final.reshape(1, D)
        res = []
        for (x1, _, g2, *st), y, sl, bb, tt in ((rp, yp, slice(0, Bp), 1, 512), (rs, ys, slice(Bp, n_c), 64, Ts)):
            B = x1.shape[0]
            if last:
                shf = mod_f[sl, :D].reshape(B, 1, D)
                scf = mod_f[sl, D:].reshape(B, 1, D)
                xo = _final(x1, y, g2, shf, scf, nf, bb, tt)
            else:
                xo = x1 + g2 * y.reshape(x1.shape)
            res.append((xo, st))
        (xp, st_p), (xs, st_s) = res
        outs_p.append(st_p)
        outs_s.append(st_s)

    def stack(outs, i):
        return jnp.stack([o[i] for o in outs])

    return (xp, xs,
            stack(outs_p, 0), stack(outs_p, 1), stack(outs_p, 2), stack(outs_p, 3), stack(outs_p, 4),
            stack(outs_s, 0), stack(outs_s, 1), stack(outs_s, 2), stack(outs_s, 3), stack(outs_s, 4))
```

```python
import functools
import math

import jax
import jax.numpy as jnp
from jax import lax
from jax.experimental import pallas as pl
from jax.experimental.pallas import tpu as pltpu

F32 = jnp.float32
BF16 = jnp.bfloat16

D_MODEL = 2048
N_HEADS = 8
D_HEAD = 128
HALF = N_HEADS * D_HEAD
CONV_W = 4
C_CONV = 3 * HALF
D_FF = 4 * D_MODEL
CHUNK = 64
EPS = 1e-6
N_GATE = 2 * N_HEADS

SUBLANES = 8
LANES = 128
VMEM_LIMIT = 56 * 1024 * 1024

NT = (((1,), (1,)), ((), ()))
TN = (((0,), (0,)), ((), ()))


def _params(sem):
    return pltpu.CompilerParams(dimension_semantics=sem, vmem_limit_bytes=VMEM_LIMIT)


def _silu(x):
    return x * jax.nn.sigmoid(x)


def _softplus(x):
    return jnp.maximum(x, 0.0) + jnp.log1p(jnp.exp(-jnp.abs(x)))


def _log_sigmoid(x):
    return jnp.minimum(x, 0.0) - jnp.log1p(jnp.exp(-jnp.abs(x)))


def _rms(x, g):
    return x * lax.rsqrt(jnp.mean(x * x, axis=-1, keepdims=True) + EPS) * g


def _bdot(a, b):
    return jnp.dot(a.astype(BF16), b.astype(BF16), preferred_element_type=F32)


def _split2(x):
    hi = x.astype(BF16)
    return hi, (x - hi.astype(F32)).astype(BF16)


def _dot3s(a2, b2):
    (ah, al), (bh, bl) = a2, b2
    d = lambda x, y: jnp.dot(x, y, preferred_element_type=F32)
    return d(ah, bh) + (d(ah, bl) + d(al, bh))


def _dot3(a, b):
    return _dot3s(_split2(a), _split2(b))


def _prefix_dot(ones_l, x, ones_r):
    hi, lo = _split2(x)
    lo2 = (x - hi.astype(F32) - lo.astype(F32)).astype(BF16)
    if ones_r is None:
        c = ones_l.astype(BF16)
        d = lambda y: jnp.dot(c, y, preferred_element_type=F32)
    else:
        c = ones_r.astype(BF16)
        d = lambda y: jnp.dot(y, c, preferred_element_type=F32)
    return d(hi) + (d(lo) + d(lo2))


def _rows(ref, r0, rows, bb):
    if bb == 1:
        return ref[0, pl.ds(r0, rows), :]
    nb = rows // SUBLANES
    b0 = pl.multiple_of(r0 // SUBLANES, nb)
    return ref[pl.ds(b0, nb), :, :].reshape(rows, ref.shape[-1])


def _mod_rows(ref, r0, rows, bb):
    if bb == 1:
        return ref[0]
    nb = rows // SUBLANES
    b0 = pl.multiple_of(r0 // SUBLANES, nb)
    v = ref[pl.ds(b0, nb), :, :]
    return jnp.broadcast_to(v, (nb, SUBLANES, v.shape[-1])).reshape(rows, v.shape[-1])


def _store_rows(ref, r0, rows, bb, val):
    if bb == 1:
        ref[0, pl.ds(r0, rows), :] = val.astype(ref.dtype)
    else:
        nb = rows // SUBLANES
        b0 = pl.multiple_of(r0 // SUBLANES, nb)
        ref[pl.ds(b0, nb), :, :] = val.reshape(nb, SUBLANES, val.shape[-1]).astype(ref.dtype)


def _tile_maps(T, tt):
    tpb = T // tt
    x_map = lambda m, *_: (m // tpb, m % tpb, 0)

    def mod_spec(bb, mref):
        _, row0, col = mref
        return pl.BlockSpec((bb, 1, D_MODEL), lambda m, *_: (row0 // bb + m // tpb, 0, col))
    return x_map, mod_spec


ROW_CHUNK = 256


def _ada_kernel(c_ref, w_ref, b_ref, o_ref):
    s = _silu(c_ref[...])
    o_ref[...] = _bdot(s, w_ref[...]) + b_ref[...]


def _ada(c, w, b, tn=1024):
    M, K = c.shape
    N = w.shape[1]
    return pl.pallas_call(
        _ada_kernel,
        grid=(N // tn,),
        in_specs=[pl.BlockSpec((M, K), lambda n: (0, 0)),
                  pl.BlockSpec((K, tn), lambda n: (0, n)),
                  pl.BlockSpec((1, tn), lambda n: (0, n))],
        out_specs=pl.BlockSpec((M, tn), lambda n: (0, n)),
        out_shape=jax.ShapeDtypeStruct((M, N), F32),
        compiler_params=_params(("arbitrary",)),
        name="ada",
    )(c, w, b.reshape(1, N))


def _inproj_kernel(x_ref, sh_ref, sc_ref, g_ref, w_ref, wg_ref, o_ref, og_ref, h_scr, *, bb, tm):
    @pl.when(pl.program_id(1) == 0)
    def _():
        def body(i, carry):
            r0 = pl.multiple_of(i * ROW_CHUNK, ROW_CHUNK)
            x = _rows(x_ref, r0, ROW_CHUNK, bb)
            h = _rms(x, g_ref[...]) * (1.0 + _mod_rows(sc_ref, r0, ROW_CHUNK, bb)) \
                + _mod_rows(sh_ref, r0, ROW_CHUNK, bb)
            h_scr[pl.ds(r0, ROW_CHUNK), :] = h.astype(BF16)
            return carry
        lax.fori_loop(0, tm // ROW_CHUNK, body, 0)
        og_ref[...] = jnp.dot(h_scr[...], wg_ref[...], preferred_element_type=F32)

    o_ref[...] = jnp.dot(h_scr[...], w_ref[...], preferred_element_type=F32).astype(o_ref.dtype)


def _inproj(x, sh, sc, g, w, wg, bb, tt, out_dtype, tn=1024):
    B, T, D = x.shape
    tm = bb * tt
    n_m = (B * T) // tm
    N = w.shape[1]
    x_map, mod_spec = _tile_maps(T, tt)
    return pl.pallas_call(
        functools.partial(_inproj_kernel, bb=bb, tm=tm),
        grid=(n_m, N // tn),
        in_specs=[pl.BlockSpec((bb, tt, D), x_map),
                  mod_spec(bb, sh), mod_spec(bb, sc),
                  pl.BlockSpec((1, D), lambda m, n: (0, 0)),
                  pl.BlockSpec((D, tn), lambda m, n: (0, n)),
                  pl.BlockSpec((D, LANES), lambda m, n: (0, 0))],
        out_specs=[pl.BlockSpec((tm, tn), lambda m, n: (m, n)),
                   pl.BlockSpec((tm, LANES), lambda m, n: (m, 0))],
        out_shape=[jax.ShapeDtypeStruct((B * T, N), out_dtype),
                   jax.ShapeDtypeStruct((B * T, LANES), F32)],
        scratch_shapes=[pltpu.VMEM((tm, D), BF16)],
        compiler_params=_params(("arbitrary", "arbitrary")),
        name="inproj",
    )(x, sh[0], sc[0], g, w, wg)


def _seq_row0(j, rows):
    return j * rows if isinstance(j, int) else pl.multiple_of(j * rows, rows)


def _tri(L):
    row = lax.broadcasted_iota(jnp.int32, (L, L), 0)
    col = lax.broadcasted_iota(jnp.int32, (L, L), 1)
    return col <= row, col < row, jnp.where(col <= row, 1.0, 0.0), jnp.where(row <= col, 1.0, 0.0)


def _head_norm_gate(hs, gs, gates):
    ms = [jnp.mean(h * h, axis=-1, keepdims=True) for h in hs]
    return [h * lax.rsqrt(v + EPS) * g * gate for h, v, g, gate in zip(hs, ms, gs, gates)]


def _for_each_group(n, fn):
    if n == 1:
        fn(0)
    else:
        def body(j, carry):
            fn(j)
            return carry
        lax.fori_loop(0, n, body, 0)


def _scan_specs(nsub, nc, L, n_steps, col0):
    rows = nsub * nc * L
    rb = lambda i, c: i * n_steps + c
    col_specs = [pl.BlockSpec((rows, HALF), functools.partial(lambda i, c, j: (rb(i, c), j), j=col0 + j))
                 for j in range(4)]
    gate_specs = [pl.BlockSpec((1, nsub, nc * L, N_GATE), lambda i, c: (rb(i, c), 0, 0, 0)),
                  pl.BlockSpec((1, nsub, nc, N_GATE, L), lambda i, c: (rb(i, c), 0, 0, 0, 0))]
    return rows, rb, col_specs, gate_specs


def _mlstm_kernel(q_ref, k_ref, v_ref, o_ref, gc_ref, gr_ref, bc_ref, br_ref, gn_ref,
                  C0_ref, n0_ref, m0_ref, h_ref, C_ref, n_ref, m_ref, *, L, nc, nsub, jb):
    @pl.when(pl.program_id(1) == 0)
    def _():
        C_ref[...] = C0_ref[...]
        n_ref[...] = n0_ref[...]
        m_ref[...] = m0_ref[...]

    tril, _, tril_f, triu_f = _tri(L)
    lane_m = lax.broadcasted_iota(jnp.int32, (1, N_HEADS), 1)
    H = range(N_HEADS)
    hs = [slice(h * D_HEAD, (h + 1) * D_HEAD) for h in H]

    TS = range(jb)
    U = [(t, h) for t in TS for h in H]

    def group(jg):
        js = [jg * jb + t for t in TS]
        m_rows = [m_ref[j] for j in js]
        for c in range(nc):
            rows = [pl.ds(_seq_row0(j, nc * L) + c * L, L) for j in js]
            zc = [gc_ref[0, j, c * L:(c + 1) * L, :] + bc_ref[...] for j in js]
            zr = [gr_ref[0, j, c] + br_ref[...] for j in js]
            lf_c = [_log_sigmoid(z[:, N_HEADS:N_GATE]) for z in zc]
            lf_r = [_log_sigmoid(z[N_HEADS:N_GATE, :]) for z in zr]
            b_c = [_prefix_dot(tril_f, x, None) for x in lf_c]
            b_r = [_prefix_dot(None, x, triu_f) for x in lf_r]
            a_c = [zc[t][:, 0:N_HEADS] - b_c[t] for t in TS]
            a_r = [zr[t][0:N_HEADS, :] - b_r[t] for t in TS]
            q = [q_ref[rows[t], hs[h]].astype(F32) for t, h in U]
            ks = [k_ref[rows[t], hs[h]].astype(F32) * (D_HEAD ** -0.5) for t, h in U]
            qb = [x.astype(BF16) for x in q]
            kb = [x.astype(BF16) for x in ks]
            vb = [v_ref[rows[t], hs[h]].astype(BF16) for t, h in U]
            N = range(len(U))
            qk = [lax.dot_general(qb[u], kb[u], NT, preferred_element_type=F32) for u in N]
            ar = [a_r[t][h:h + 1, :] for t, h in U]
            A_c = [jnp.max(jnp.where(tril, ar[u], -jnp.inf), axis=1, keepdims=True) for u in N]
            p0 = [qk[u] * jnp.exp(jnp.where(tril, ar[u] - A_c[u], -jnp.inf)) for u in N]
            rs = [jnp.sum(p0[u], axis=1, keepdims=True) for u in N]
            sv = [jnp.dot(p0[u].astype(BF16), vb[u], preferred_element_type=F32) for u in N]
            A_last = [A_c[u][L - 1:L, :] for u in N]
            kw = [ks[u] * jnp.exp(a_c[t][:, h:h + 1] - A_last[u]) for u, (t, h) in enumerate(U)]
            kv = [lax.dot_general(kw[u].astype(BF16), vb[u], TN, preferred_element_type=F32) for u in N]
            ksum = [jnp.sum(kw[u], axis=0, keepdims=True) for u in N]
            C = [C_ref[js[t], h] for t, h in U]
            n = [n_ref[js[t], h:h + 1, :] for t, h in U]
            m = [m_rows[t][:, h:h + 1] for t, h in U]
            bc = [b_c[t][:, h:h + 1] for t, h in U]
            qC = [jnp.dot(qb[u], C[u].astype(BF16), preferred_element_type=F32) for u in N]
            M_c = [jnp.maximum(m[u], A_c[u]) for u in N]
            r = [jnp.exp(A_c[u] - M_c[u]) for u in N]
            dec = [jnp.exp(m[u] - M_c[u]) for u in N]
            qn = [jnp.sum(q[u] * n[u], axis=1, keepdims=True) for u in N]
            numer = [dec[u] * qC[u] + r[u] * sv[u] for u in N]
            dd = [dec[u] * qn[u] + r[u] * rs[u] for u in N]
            floor = [jnp.exp(-(bc[u] + M_c[u])) for u in N]
            hh = [numer[u] / jnp.maximum(jnp.abs(dd[u]), floor[u]) for u in N]
            M_last = [M_c[u][L - 1:L, :] for u in N]
            rl = [jnp.exp(A_last[u] - M_last[u]) for u in N]
            dC = [jnp.exp(m[u] - M_last[u]) for u in N]
            gate = [jax.nn.sigmoid(o_ref[rows[t], hs[h]].astype(F32)) for t, h in U]
            out = _head_norm_gate(hh, [gn_ref[:, hs[h]] for t, h in U], gate)
            for u, (t, h) in enumerate(U):
                C_ref[js[t], h] = dC[u] * C[u] + rl[u] * kv[u]
                n_ref[js[t], h:h + 1, :] = dC[u] * n[u] + rl[u] * ksum[u]
                m_rows[t] = jnp.where(lane_m == h, bc[u][L - 1:L, :] + M_last[u], m_rows[t])
                h_ref[rows[t], hs[h]] = out[u].astype(h_ref.dtype)
        for t in TS:
            m_ref[js[t]] = m_rows[t]

    _for_each_group(nsub // jb, group)


def _mlstm(proj, gc, gr, bias, gn, C0, n0, m0, L, nc, nsub, jb, act_dtype):
    B = C0.shape[0]
    n_tok = proj.shape[0]
    n_bblk = B // nsub
    n_steps = n_tok // (n_bblk * nsub * nc * L)
    rows, rb, col_specs, gate_specs = _scan_specs(nsub, nc, L, n_steps, 0)
    const2 = lambda i, c: (0, 0)
    st4 = pl.BlockSpec((nsub, N_HEADS, D_HEAD, D_HEAD), lambda i, c: (i, 0, 0, 0))
    st3 = pl.BlockSpec((nsub, N_HEADS, D_HEAD), lambda i, c: (i, 0, 0))
    stm = pl.BlockSpec((nsub, 1, N_HEADS), lambda i, c: (i, 0, 0))
    return pl.pallas_call(
        functools.partial(_mlstm_kernel, L=L, nc=nc, nsub=nsub, jb=jb),
        grid=(n_bblk, n_steps),
        in_specs=col_specs + gate_specs + [
            pl.BlockSpec((1, N_GATE), const2), pl.BlockSpec((N_GATE, 1), const2),
            pl.BlockSpec((1, HALF), const2), st4, st3, stm],
        out_specs=[pl.BlockSpec((rows, HALF), lambda i, c: (rb(i, c), 0)), st4, st3, stm],
        out_shape=[jax.ShapeDtypeStruct((n_tok, HALF), act_dtype),
                   jax.ShapeDtypeStruct(C0.shape, F32),
                   jax.ShapeDtypeStruct(n0.shape, F32),
                   jax.ShapeDtypeStruct(m0.shape, F32)],
        compiler_params=_params(("arbitrary", "arbitrary")),
        name="mlstm",
    )(proj, proj, proj, proj, gc, gr, bias.reshape(1, N_GATE), bias.reshape(N_GATE, 1),
      gn.reshape(1, HALF), C0, n0, m0)


def _conv_silu(u, prev, w_ref, cs, first_rows):
    R = u.shape[0]
    out = u * w_ref[CONV_W - 1:CONV_W, cs]
    for j in range(1, CONV_W):
        shifted = pltpu.roll(u, j, 0)
        head = jnp.where(first_rows < j, pltpu.roll(prev, j, 0), shifted[0:SUBLANES, :])
        if R > SUBLANES:
            shifted = jnp.concatenate([head, shifted[SUBLANES:, :]], axis=0)
        else:
            shifted = head
        out = out + shifted * w_ref[CONV_W - 1 - j:CONV_W - j, cs]
    return _silu(out)


def _l2n(x):
    return x * lax.rsqrt(jnp.sum(x * x, axis=-1, keepdims=True) + EPS)


def _unit_lower_inverse(mats):
    L = mats[0].shape[0]
    B8 = SUBLANES
    n = range(len(mats))
    lane = lax.broadcasted_iota(jnp.int32, (B8, L), 1)
    sub = lax.broadcasted_iota(jnp.int32, (B8, L), 0)
    blk = lane // B8
    packed = []
    for a in mats:
        p = a[0:B8, :]
        for b in range(1, L // B8):
            p = jnp.where(blk == b, a[b * B8:(b + 1) * B8, :], p)
        packed.append(p)
    eye8 = jnp.where(lane - blk * B8 == sub, 1.0, 0.0)
    x = [eye8 for _ in n]
    for s in range(B8 - 1):
        cols = [jnp.take_along_axis(packed[i], blk * B8 + s, axis=1) for i in n]
        x = [x[i] - cols[i] * x[i][s:s + 1, :] for i in n]
    if L == B8:
        return x
    d = [jnp.concatenate([jnp.where(blk == b, x[i], 0.0) for b in range(L // B8)], axis=0) for i in n]
    row = lax.broadcasted_iota(jnp.int32, (L, L), 0)
    colm = lax.broadcasted_iota(jnp.int32, (L, L), 1)
    a2 = [_split2(a) for a in mats]
    size = B8
    while size < L:
        rb, cb = row // size, colm // size
        join = jnp.where((rb // 2 == cb // 2) & (rb != cb), 1.0, 0.0).astype(BF16)
        d2 = [_split2(x) for x in d]
        ad = [_dot3s((a2[i][0] * join, a2[i][1] * join), d2[i]) for i in n]
        dad = [_dot3s(d2[i], _split2(ad[i])) for i in n]
        d = [d[i] - dad[i] for i in n]
        size *= 2
    return d


def _gdn_kernel(q_ref, k_ref, v_ref, z_ref, gc_ref, gr_ref, ac_ref, ar_ref, dc_ref, dr_ref,
                cw_ref, gn_ref, S0_ref, cv0_ref, h_ref, S_ref, cv1_ref,
                qkv_scr, wq_scr, u_scr, qk_scr, kd_scr, gs_scr, cv_ref, *, L, nc, nsub, jb):
    tail = slice(SUBLANES - (CONV_W - 1), SUBLANES)

    @pl.when(pl.program_id(1) == 0)
    def _():
        S_ref[...] = S0_ref[...]
        cv_ref[...] = jnp.zeros_like(cv_ref)
        cv_ref[:, tail, :] = cv0_ref[...]

    R = nc * L
    tril, strict, tril_f, triu_f = _tri(L)
    first_rows = lax.broadcasted_iota(jnp.int32, (SUBLANES, HALF), 0)
    H = range(N_HEADS)
    hs = [slice(h * D_HEAD, (h + 1) * D_HEAD) for h in H]

    TS = range(jb)
    U = [(t, h) for t in TS for h in H]
    N = range(len(U))

    def group(jg):
        js = [jg * jb + t for t in TS]
        r0 = [_seq_row0(j, R) for j in js]
        for t in TS:
            for p, ref in enumerate((q_ref, k_ref, v_ref)):
                cs = slice(p * HALF, (p + 1) * HALF)
                u = ref[pl.ds(r0[t], R), :].astype(F32)
                prev = cv_ref[js[t], :, cs]
                qkv_scr[t * 3 + p] = _conv_silu(u, prev, cw_ref, cs, first_rows)
                cv_ref[js[t], :, cs] = u[R - SUBLANES:, :]
        for c in range(nc):
            rc = slice(c * L, (c + 1) * L)
            zc = [gc_ref[0, j, rc, :] for j in js]
            zr = [gr_ref[0, j, c] for j in js]
            g_c = [-jnp.exp(ac_ref[...]) * _softplus(z[:, 0:N_HEADS] + dc_ref[...]) for z in zc]
            g_r = [-jnp.exp(ar_ref[...]) * _softplus(z[0:N_HEADS, :] + dr_ref[...]) for z in zr]
            beta_t = [jax.nn.sigmoid(z[:, N_HEADS:N_GATE]) for z in zc]
            G_c = [_prefix_dot(tril_f, x, None) for x in g_c]
            G_r = [_prefix_dot(None, x, triu_f) for x in g_r]
            gam_t = [jnp.exp(x) for x in G_c]
            G_last = [x[L - 1:L, :] for x in G_c]
            kdec_t = [jnp.exp(G_last[t] - G_c[t]) for t in TS]
            for t in TS:
                gs_scr[t * nc + c] = jnp.exp(G_last[t])
            beta = [beta_t[t][:, h:h + 1] for t, h in U]
            gam = [gam_t[t][:, h:h + 1] for t, h in U]
            qn = [_l2n(qkv_scr[t * 3, rc, hs[h]]) * (D_HEAD ** -0.5) for t, h in U]
            kn = [_l2n(qkv_scr[t * 3 + 1, rc, hs[h]]) for t, h in U]
            kb = [x.astype(BF16) for x in kn]
            kq = [lax.dot_general(jnp.concatenate([kn[u], qn[u]], axis=0).astype(BF16), kb[u], NT,
                                  preferred_element_type=F32) for u in N]
            decay = [jnp.exp(jnp.where(tril, G_c[t][:, h:h + 1] - G_r[t][h:h + 1, :], -jnp.inf)) for t, h in U]
            a = [jnp.where(strict, beta[u] * decay[u] * kq[u][0:L, :], 0.0) for u in N]
            for u, (t, h) in enumerate(U):
                qk_scr[t * nc + c, h] = kq[u][L:2 * L, :] * decay[u]
                kd_scr[t * nc + c, h] = kn[u] * kdec_t[t][:, h:h + 1]
            T = _unit_lower_inverse(a)
            rhs = [jnp.concatenate([beta[u] * gam[u] * kn[u], beta[u] * qkv_scr[t * 3 + 2, rc, hs[h]]], axis=1)
                   for u, (t, h) in enumerate(U)]
            wu = [_dot3(T[u], rhs[u]) for u in N]
            for u, (t, h) in enumerate(U):
                wq_scr[t * nc + c, h] = jnp.concatenate([wu[u][:, 0:D_HEAD], gam[u] * qn[u]], axis=0).astype(BF16)
                u_scr[t * nc + c, h] = wu[u][:, D_HEAD:]
        for c in range(nc):
            rows = [pl.ds(r0[t] + c * L, L) for t in TS]
            S = [S_ref[js[t], h] for t, h in U]
            gS = [gs_scr[t * nc + c] for t in TS]
            P = [jnp.dot(wq_scr[t * nc + c, h], S[u].astype(BF16), preferred_element_type=F32)
                 for u, (t, h) in enumerate(U)]
            ub = [(u_scr[t * nc + c, h] - P[u][0:L, :]).astype(BF16) for u, (t, h) in enumerate(U)]
            o = [P[u][L:2 * L, :] + jnp.dot(qk_scr[t * nc + c, h].astype(BF16), ub[u], preferred_element_type=F32)
                 for u, (t, h) in enumerate(U)]
            dS = [lax.dot_general(kd_scr[t * nc + c, h].astype(BF16), ub[u], TN, preferred_element_type=F32)
                  for u, (t, h) in enumerate(U)]
            gate = [_silu(z_ref[rows[t], hs[h]].astype(F32)) for t, h in U]
            out = _head_norm_gate(o, [gn_ref[:, hs[h]] for t, h in U], gate)
            for u, (t, h) in enumerate(U):
                S_ref[js[t], h] = gS[t][:, h:h + 1] * S[u] + dS[u]
                h_ref[rows[t], hs[h]] = out[u].astype(h_ref.dtype)

    _for_each_group(nsub // jb, group)
    cv1_ref[...] = cv_ref[:, tail, :]


def _gdn(proj, gc, gr, A_log, dt_bias, conv_w, gn, S0, cv0, L, nc, nsub, jb, act_dtype):
    B = S0.shape[0]
    n_tok = proj.shape[0]
    n_bblk = B // nsub
    n_steps = n_tok // (n_bblk * nsub * nc * L)
    rows, rb, col_specs, gate_specs = _scan_specs(nsub, nc, L, n_steps, 4)
    const2 = lambda i, c: (0, 0)
    st4 = pl.BlockSpec((nsub, N_HEADS, D_HEAD, D_HEAD), lambda i, c: (i, 0, 0, 0))
    stc = pl.BlockSpec((nsub, CONV_W - 1, C_CONV), lambda i, c: (i, 0, 0))
    return pl.pallas_call(
        functools.partial(_gdn_kernel, L=L, nc=nc, nsub=nsub, jb=jb),
        grid=(n_bblk, n_steps),
        in_specs=col_specs + gate_specs + [
            pl.BlockSpec((1, N_HEADS), const2), pl.BlockSpec((N_HEADS, 1), const2),
            pl.BlockSpec((1, N_HEADS), const2), pl.BlockSpec((N_HEADS, 1), const2),
            pl.BlockSpec((CONV_W, C_CONV), const2), pl.BlockSpec((1, HALF), const2), st4, stc],
        out_specs=[pl.BlockSpec((rows, HALF), lambda i, c: (rb(i, c), 0)), st4, stc],
        out_shape=[jax.ShapeDtypeStruct((n_tok, HALF), act_dtype),
                   jax.ShapeDtypeStruct(S0.shape, F32),
                   jax.ShapeDtypeStruct(cv0.shape, F32)],
        scratch_shapes=[pltpu.VMEM((jb * 3, nc * L, HALF), F32),
                        pltpu.VMEM((jb * nc, N_HEADS, 2 * L, D_HEAD), BF16),
                        pltpu.VMEM((jb * nc, N_HEADS, L, D_HEAD), F32),
                        pltpu.VMEM((jb * nc, N_HEADS, L, L), F32),
                        pltpu.VMEM((jb * nc, N_HEADS, L, D_HEAD), F32),
                        pltpu.VMEM((jb * nc, 1, N_HEADS), F32),
                        pltpu.VMEM((nsub, SUBLANES, C_CONV), F32)],
        compiler_params=_params(("arbitrary", "arbitrary")),
        name="gdn",
    )(proj, proj, proj, proj, gc, gr,
      A_log.reshape(1, N_HEADS), A_log.reshape(N_HEADS, 1),
      dt_bias.reshape(1, N_HEADS), dt_bias.reshape(N_HEADS, 1),
      conv_w, gn.reshape(1, HALF), S0, cv0)


def _outproj_kernel(ha_ref, hb_ref, wa_ref, wb_ref, x_ref, g1_ref, sh_ref, sc_ref, gn_ref,
                    x1_ref, h2_ref, *, bb, tm):
    def body(i, carry):
        r0 = pl.multiple_of(i * ROW_CHUNK, ROW_CHUNK)
        mix = _bdot(ha_ref[pl.ds(r0, ROW_CHUNK), :], wa_ref[...]) \
            + _bdot(hb_ref[pl.ds(r0, ROW_CHUNK), :], wb_ref[...])
        x1 = _rows(x_ref, r0, ROW_CHUNK, bb) + _mod_rows(g1_ref, r0, ROW_CHUNK, bb) * mix
        _store_rows(x1_ref, r0, ROW_CHUNK, bb, x1)
        h2 = _rms(x1, gn_ref[...]) * (1.0 + _mod_rows(sc_ref, r0, ROW_CHUNK, bb)) \
            + _mod_rows(sh_ref, r0, ROW_CHUNK, bb)
        h2_ref[pl.ds(r0, ROW_CHUNK), :] = h2.astype(BF16)
        return carry
    lax.fori_loop(0, tm // ROW_CHUNK, body, 0)


def _outproj(ha, hb, w_out, x, g1, sh2, sc2, gn, bb, tt):
    B, T, D = x.shape
    tm = bb * tt
    x_map, mod_spec = _tile_maps(T, tt)
    return pl.pallas_call(
        functools.partial(_outproj_kernel, bb=bb, tm=tm),
        grid=((B * T) // tm,),
        in_specs=[pl.BlockSpec((tm, HALF), lambda m: (m, 0)),
                  pl.BlockSpec((tm, HALF), lambda m: (m, 0)),
                  pl.BlockSpec((HALF, D), lambda m: (0, 0)),
                  pl.BlockSpec((HALF, D), lambda m: (1, 0)),
                  pl.BlockSpec((bb, tt, D), x_map),
                  mod_spec(bb, g1), mod_spec(bb, sh2), mod_spec(bb, sc2),
                  pl.BlockSpec((1, D), lambda m: (0, 0))],
        out_specs=[pl.BlockSpec((bb, tt, D), x_map),
                   pl.BlockSpec((tm, D), lambda m: (m, 0))],
        out_shape=[jax.ShapeDtypeStruct((B, T, D), F32),
                   jax.ShapeDtypeStruct((B * T, D), BF16)],
        compiler_params=_params(("arbitrary",)),
        name="outproj",
    )(ha, hb, w_out, w_out, x, g1[0], sh2[0], sc2[0], gn)


def _mlp_kernel(h_ref, wu_ref, wd_ref, o_ref, *wb_refs, tm):
    if wb_refs:
        wu_scr, wd_scr = wb_refs
        wu_scr[...] = wu_ref[...].astype(BF16)
        wd_scr[...] = wd_ref[...].astype(BF16)
    else:
        wu_scr, wd_scr = wu_ref, wd_ref

    @pl.when(pl.program_id(1) == 0)
    def _():
        o_ref[...] = jnp.zeros_like(o_ref)

    def body(i, carry):
        r0 = pl.multiple_of(i * ROW_CHUNK, ROW_CHUNK)
        a = jnp.dot(h_ref[pl.ds(r0, ROW_CHUNK), :], wu_scr[...], preferred_element_type=F32)
        a = jnp.square(jnp.maximum(a, 0.0)).astype(BF16)
        o_ref[pl.ds(r0, ROW_CHUNK), :] += jnp.dot(a, wd_scr[...], preferred_element_type=F32)
        return carry
    lax.fori_loop(0, tm // ROW_CHUNK, body, 0, unroll=True)


def _mlp(h2, w_up, w_down, tm, tf):
    n_tok, D = h2.shape
    emit_bf16 = w_up.dtype != BF16
    assert not emit_bf16 or n_tok == tm
    wu_spec = pl.BlockSpec((D, tf), lambda m, f: (0, f))
    wd_spec = pl.BlockSpec((tf, D), lambda m, f: (f, 0))
    y_spec = pl.BlockSpec((tm, D), lambda m, f: (m, 0))
    y_shape = jax.ShapeDtypeStruct((n_tok, D), F32)
    return pl.pallas_call(
        functools.partial(_mlp_kernel, tm=tm),
        grid=(n_tok // tm, D_FF // tf),
        in_specs=[pl.BlockSpec((tm, D), lambda m, f: (m, 0)), wu_spec, wd_spec],
        out_specs=[y_spec, wu_spec, wd_spec] if emit_bf16 else y_spec,
        out_shape=[y_shape, jax.ShapeDtypeStruct(w_up.shape, BF16), jax.ShapeDtypeStruct(w_down.shape, BF16)]
        if emit_bf16 else y_shape,
        compiler_params=_params(("arbitrary", "arbitrary")),
        name="mlp",
    )(h2, w_up, w_down)


def _final_kernel(x1_ref, y_ref, g2_ref, sh_ref, sc_ref, gn_ref, o_ref, *, bb, tm):
    def body(i, carry):
        r0 = pl.multiple_of(i * ROW_CHUNK, ROW_CHUNK)
        x2 = _rows(x1_ref, r0, ROW_CHUNK, bb) + _mod_rows(g2_ref, r0, ROW_CHUNK, bb) * y_ref[pl.ds(r0, ROW_CHUNK), :]
        out = _rms(x2, gn_ref[...]) * (1.0 + _mod_rows(sc_ref, r0, ROW_CHUNK, bb)) \
            + _mod_rows(sh_ref, r0, ROW_CHUNK, bb)
        _store_rows(o_ref, r0, ROW_CHUNK, bb, out)
        return carry
    lax.fori_loop(0, tm // ROW_CHUNK, body, 0)


def _final(x1, y, g2, shf, scf, gn, bb, tt):
    B, T, D = x1.shape
    tm = bb * tt
    x_map, mod_spec = _tile_maps(T, tt)
    return pl.pallas_call(
        functools.partial(_final_kernel, bb=bb, tm=tm),
        grid=((B * T) // tm,),
        in_specs=[pl.BlockSpec((bb, tt, D), x_map),
                  pl.BlockSpec((tm, D), lambda m: (m, 0)),
                  mod_spec(bb, g2), mod_spec(bb, shf), mod_spec(bb, scf),
                  pl.BlockSpec((1, D), lambda m: (0, 0))],
        out_specs=pl.BlockSpec((bb, tt, D), x_map),
        out_shape=jax.ShapeDtypeStruct((B, T, D), F32),
        compiler_params=_params(("arbitrary",)),
        name="final",
    )(x1, y, g2[0], shf[0], scf[0], gn)


def _gate_views(gates, off, n_blk, nsub, nc, L):
    g = gates[:, off:off + N_GATE].reshape(n_blk, nsub, nc, L, N_GATE)
    return g.reshape(n_blk, nsub, nc * L, N_GATE), jnp.swapaxes(g, 3, 4)


def _group_layer(x, mod, row0, C0, n0, m0, S0, cv0, w, bb_big, tt_big, bb_small, tt_small, nc, nsub, jb, act_dtype):
    B, T, D = x.shape
    L = math.gcd(T, CHUNK)
    sh1, sc1, g1, sh2, sc2, g2 = [(mod, row0, i) for i in range(6)]
    proj, gates = _inproj(x, sh1, sc1, w["norm1"], w["w_in_main"], w["w_in_gate"], bb_big, tt_big, act_dtype)
    n_blk = (B * T) // (nsub * nc * L)
    gca, gra = _gate_views(gates, 0, n_blk, nsub, nc, L)
    gcb, grb = _gate_views(gates, N_GATE, n_blk, nsub, nc, L)
    ha, C1, n1, m1 = _mlstm(proj, gca, gra, w["gate_b"], w["mlstm_g"], C0, n0,
                            m0.reshape(B, 1, N_HEADS), L, nc, nsub, jb, act_dtype)
    hb, S1, cv1 = _gdn(proj, gcb, grb, w["A_log"], w["dt_bias"], w["conv_w"], w["gdn_g"], S0, cv0,
                       L, nc, nsub, jb, act_dtype)
    x1, h2 = _outproj(ha, hb, w["w_out"], x, g1, sh2, sc2, w["norm2"], bb_small, tt_small)
    return x1, h2, g2, C1, n1, m1.reshape(B, N_HEADS), S1, cv1


GATE_AFTER = (3, 7)
REGROUP_ROWS = 512


def _regroup_kernel(a_ref, b_ref, o_ref):
    j = pl.program_id(0)
    x = jnp.concatenate([a_ref[...], b_ref[...]], axis=1)
    bounds = (0,) + GATE_AFTER + (2 * N_HEADS,)
    for g in range(len(bounds) - 1):
        @pl.when((j >= bounds[g]) & (j < bounds[g + 1]))
        def _(s=g * N_GATE):
            o_ref[...] = x[:, s:s + HALF].astype(BF16)


def _regroup_w_in(w_in):
    K = w_in.shape[0]
    n_wide = 8
    main = pl.pallas_call(
        _regroup_kernel,
        grid=(n_wide, K // REGROUP_ROWS),
        in_specs=[pl.BlockSpec((REGROUP_ROWS, HALF), lambda j, k: (k, j)),
                  pl.BlockSpec((REGROUP_ROWS, LANES), lambda j, k: (k, (j + 1) * (HALF // LANES)))],
        out_specs=pl.BlockSpec((REGROUP_ROWS, HALF), lambda j, k: (k, j)),
        out_shape=jax.ShapeDtypeStruct((K, n_wide * HALF), BF16),
        compiler_params=_params(("arbitrary", "arbitrary")),
        name="regroup",
    )(w_in, w_in)
    g0 = GATE_AFTER[0] * HALF
    g1 = GATE_AFTER[1] * HALF + N_GATE
    gate = jnp.concatenate([w_in[:, g0:g0 + N_GATE], w_in[:, g1:g1 + N_GATE],
                            jnp.zeros((K, LANES - 2 * N_GATE), w_in.dtype)], axis=1)
    return main, gate.astype(BF16)


def kernel(x_prompt, x_sample, state_mlstm_C, state_mlstm_n, state_mlstm_m, state_gdn_S, state_gdn_conv,
           c_prompt, c_sample, ada_w, ada_b, norm1, w_in, mlstm_gate_bias, mlstm_norm, gdn_conv_w,
           gdn_A_log, gdn_dt_bias, gdn_norm, w_out, norm2, w_up, w_down, ada_final_w, ada_final_b,
           norm_final):
    Bp, Tp, D = x_prompt.shape
    Bs, Ts, _ = x_sample.shape
    assert ada_w.shape[0] == 1, "single-layer step"
    n_c = Bs + Bp
    pad = (-n_c) % SUBLANES
    c_all = jnp.concatenate([c_sample, c_prompt, jnp.zeros((pad, D), F32)], axis=0)
    n_rows = n_c + pad
    mod_f = _ada(c_all, ada_final_w, ada_final_b).reshape(n_rows, 1, 2 * D)
    mod = _ada(c_all, ada_w[0], ada_b[0]).reshape(n_rows, 1, 6 * D)

    main, gate = _regroup_w_in(w_in[0])
    w = dict(norm1=norm1[0].reshape(1, D), w_in_main=main, w_in_gate=gate, gate_b=mlstm_gate_bias[0],
             mlstm_g=mlstm_norm[0], conv_w=gdn_conv_w[0], A_log=gdn_A_log[0], dt_bias=gdn_dt_bias[0],
             gdn_g=gdn_norm[0], w_out=w_out[0].astype(BF16), norm2=norm2[0].reshape(1, D))
    zC = jnp.zeros((Bp, N_HEADS, D_HEAD, D_HEAD), F32)
    rp = _group_layer(x_prompt, mod, Bs, zC, jnp.zeros((Bp, N_HEADS, D_HEAD), F32), jnp.zeros((Bp, N_HEADS), F32),
                      zC, jnp.zeros((Bp, CONV_W - 1, C_CONV), F32), w,
                      1, 1024, 1, 512, 2, 1, 1, BF16)
    rs = _group_layer(x_sample, mod, 0, state_mlstm_C[0], state_mlstm_n[0], state_mlstm_m[0],
                      state_gdn_S[0], state_gdn_conv[0], w,
                      Bs, Ts, 64, Ts, 1, 8, 8, F32)
    ys, wu_b, wd_b = _mlp(rs[1], w_up[0], w_down[0], Bs * Ts, 512)
    yp = _mlp(rp[1], wu_b, wd_b, 1024, 1024)
    nf = norm_final.reshape(1, D)
    outs = []
    for (x1, _, g2, *st), y, row0, bb, tt in ((rp, yp, Bs, 1, 512), (rs, ys, 0, 64, Ts)):
        outs.append((_final(x1, y, g2, (mod_f, row0, 0), (mod_f, row0, 1), nf, bb, tt), st))
    (yp_out, st_p), (ys_out, st_s) = outs
    return (yp_out, ys_out) + tuple(s[None] for s in st_p) + tuple(s[None] for s in st_s)
```
